```python
import jax, jax.numpy as jnp
from jax import lax
import numpy as np

D_MODEL = 1024
BATCH = 2
SEQ = 8192
DEPTH = 4

N_EVEN = (DEPTH + 1) // 2
N_ODD = DEPTH // 2
EPS = 1e-6
NEG_INF = -1e30
Q_BLOCK = 128
A_HEADS = 4
A_HEAD_DIM = 128
A_CHUNK = 128
A_WIDTH = A_HEADS * A_HEAD_DIM
B_HEADS = 8
B_HEAD_DIM = 64
B_WIDTH = B_HEADS * B_HEAD_DIM
B_BRANCHES = ((128, 1), (512, 4), (2048, 16))
EVEN_IN = 2 * A_WIDTH + 3 * B_WIDTH
MIX_WIDTH = A_WIDTH + B_WIDTH
C_HEADS = 8
C_NOPE = 128
C_ROPE = 64
C_V = 128
C_Q_LORA = 512
C_KV_LORA = 256
ROPE_THETA = 10000.0
ODD_IN = C_Q_LORA + C_KV_LORA + C_ROPE
N_EXPERTS = 32
TOP_K = 4
D_EXPERT = 1024
SWIGLU_ALPHA = 1.702
SWIGLU_LIMIT = 7.0
MOE_BLOCK = 256

kernel_name = "hybrid_gmlp_dilated_mla_moe_trunk"


def rms_norm(x, g):
    xf = x.astype(jnp.float32)
    y = xf * lax.rsqrt(jnp.mean(xf * xf, axis=-1, keepdims=True) + EPS)
    return (y * g.astype(jnp.float32)).astype(x.dtype)


def rope(x, pos):
    half = x.shape[-1] // 2
    inv = ROPE_THETA ** (-jnp.arange(half, dtype=jnp.float32) / half)
    ang = pos.astype(jnp.float32)[:, None] * inv[None, :]
    cos = jnp.cos(ang)[:, None, :]
    sin = jnp.sin(ang)[:, None, :]
    xf = x.astype(jnp.float32)
    x1, x2 = xf[..., :half], xf[..., half:]
    return jnp.concatenate([x1 * cos - x2 * sin, x1 * sin + x2 * cos], axis=-1).astype(x.dtype)


def chunked_gating_mlp(u, v, w_s, b_s, g_v):
    bsz, s, h, dh = v.shape
    v = rms_norm(v, g_v)
    vc = v.reshape(bsz, s // A_CHUNK, A_CHUNK, h, dh)
    causal = jnp.tril(jnp.ones((A_CHUNK, A_CHUNK), dtype=bool))
    w = jnp.where(causal[None], w_s, 0.0)
    mixed = jnp.einsum('hts,bnshd->bnthd', w, vc) + b_s.T[None, None, :, :, None]
    return u * mixed.reshape(bsz, s, h, dh)


def dilated_window_attention(q, k, v):
    bsz, s, h, dh = q.shape
    nb = s // Q_BLOCK
    scale = dh ** -0.5
    qb = q.reshape(bsz, nb, Q_BLOCK, h, dh).swapaxes(0, 1)

    def block(args):
        q_blk, i = args
        t = i * Q_BLOCK + jnp.arange(Q_BLOCK)
        outs = []
        lses = []
        for window, dil in B_BRANCHES:
            n_keys = window // dil + 1
            pos = t[:, None] - dil * jnp.arange(n_keys)[None, :]
            valid = pos >= 0
            pos_c = jnp.maximum(pos, 0)
            kg = jnp.take(k, pos_c, axis=1)
            vg = jnp.take(v, pos_c, axis=1)
            logits = jnp.einsum('bqhd,bqkhd->bhqk', q_blk, kg).astype(jnp.float32) * scale
            logits = jnp.where(valid[None, None], logits, NEG_INF)
            lse = jax.nn.logsumexp(logits, axis=-1)
            p = jnp.exp(logits - lse[..., None])
            outs.append(jnp.einsum('bhqk,bqkhd->bqhd', p.astype(v.dtype), vg))
            lses.append(lse)
        wts = jax.nn.softmax(jnp.stack(lses, axis=0), axis=0)
        wts = wts.transpose(0, 1, 3, 2)[..., None].astype(v.dtype)
        return jnp.sum(wts * jnp.stack(outs, axis=0), axis=0)

    out = lax.map(block, (qb, jnp.arange(nb)))
    return out.swapaxes(0, 1).reshape(bsz, s, h, dh)


def causal_block_attention(q, k, v):
    bsz, s, h, dq = q.shape
    nb = s // Q_BLOCK
    scale = dq ** -0.5
    qb = q.reshape(bsz, nb, Q_BLOCK, h, dq).swapaxes(0, 1)
    kpos = jnp.arange(s)

    def block(args):
        q_blk, i = args
        qpos = i * Q_BLOCK + jnp.arange(Q_BLOCK)
        sc = jnp.einsum('bqhd,bkhd->bhqk', q_blk, k).astype(jnp.float32) * scale
        sc = jnp.where(kpos[None, :] <= qpos[:, None], sc, NEG_INF)
        p = jax.nn.softmax(sc, axis=-1)
        return jnp.einsum('bhqk,bkhd->bqhd', p.astype(v.dtype), v)

    out = lax.map(block, (qb, jnp.arange(nb)))
    return out.swapaxes(0, 1).reshape(bsz, s, h, v.shape[-1])


def even_mixer(h, w_in, w_s, b_s, g_v, g_q, g_k, w_out):
    bsz, s, _ = h.shape
    z = h @ w_in
    zu, zv, q, k, v = jnp.split(z, [A_WIDTH, 2 * A_WIDTH, 2 * A_WIDTH + B_WIDTH, 2 * A_WIDTH + 2 * B_WIDTH], axis=-1)
    u = jax.nn.gelu(zu, approximate=False).reshape(bsz, s, A_HEADS, A_HEAD_DIM)
    vv = jax.nn.gelu(zv, approximate=False).reshape(bsz, s, A_HEADS, A_HEAD_DIM)
    a_out = chunked_gating_mlp(u, vv, w_s, b_s, g_v)
    q = rms_norm(q.reshape(bsz, s, B_HEADS, B_HEAD_DIM), g_q)
    k = rms_norm(k.reshape(bsz, s, B_HEADS, B_HEAD_DIM), g_k)
    v = v.reshape(bsz, s, B_HEADS, B_HEAD_DIM)
    b_out = dilated_window_attention(q, k, v)
    mixed = jnp.concatenate([a_out.reshape(bsz, s, A_WIDTH), b_out.reshape(bsz, s, B_WIDTH)], axis=-1)
    return mixed @ w_out


def mla_mixer(h, w_in, g_cq, g_ckv, w_uq, w_ukv, g_q, g_k, w_out):
    bsz, s, _ = h.shape
    z = h @ w_in
    c_q, c_kv, k_r = jnp.split(z, [C_Q_LORA, C_Q_LORA + C_KV_LORA], axis=-1)
    c_q = rms_norm(c_q, g_cq)
    c_kv = rms_norm(c_kv, g_ckv)
    q = (c_q @ w_uq).reshape(bsz, s, C_HEADS, C_NOPE + C_ROPE)
    kv = (c_kv @ w_ukv).reshape(bsz, s, C_HEADS, C_NOPE + C_V)
    k_nope, v = kv[..., :C_NOPE], kv[..., C_NOPE:]
    k_rope = jnp.broadcast_to(k_r[:, :, None, :], (bsz, s, C_HEADS, C_ROPE))
    k = jnp.concatenate([k_nope, k_rope], axis=-1)
    q = rms_norm(q, g_q)
    k = rms_norm(k, g_k)
    pos = jnp.arange(s)
    q = jnp.concatenate([q[..., :C_NOPE], rope(q[..., C_NOPE:], pos)], axis=-1)
    k = jnp.concatenate([k[..., :C_NOPE], rope(k[..., C_NOPE:], pos)], axis=-1)
    o = causal_block_attention(q, k, v)
    return o.reshape(bsz, s, C_HEADS * C_V) @ w_out


def moe_ffn(h, w_router, b_router, w1, b1, w2, b2):
    bsz, s, d = h.shape
    x = h.reshape(-1, d)
    n_tok = x.shape[0]
    n_assign = n_tok * TOP_K
    logits = (x @ w_router + b_router).astype(jnp.float32)
    top_val, top_idx = lax.top_k(logits, TOP_K)
    gates = jax.nn.softmax(top_val, axis=-1)
    e_flat = top_idx.reshape(-1)
    tok_flat = jnp.repeat(jnp.arange(n_tok, dtype=jnp.int32), TOP_K)
    g_flat = gates.reshape(-1)
    counts = jnp.bincount(e_flat, length=N_EXPERTS)
    padded = (counts + MOE_BLOCK - 1) // MOE_BLOCK * MOE_BLOCK
    pad_end = jnp.cumsum(padded)
    pad_start = pad_end - padded
    start = jnp.cumsum(counts) - counts
    order = jnp.argsort(e_flat)
    e_sorted = e_flat[order]
    rank = jnp.arange(n_assign) - start[e_sorted]
    slot = pad_start[e_sorted] + rank
    n_blocks = -(-(n_assign + N_EXPERTS * (MOE_BLOCK - 1)) // MOE_BLOCK)
    n_rows = n_blocks * MOE_BLOCK
    row_tok = jnp.zeros((n_rows,), jnp.int32).at[slot].set(tok_flat[order])
    row_gate = jnp.zeros((n_rows,), jnp.float32).at[slot].set(g_flat[order])
    block_start = jnp.arange(n_blocks) * MOE_BLOCK
    block_expert = jnp.minimum(jnp.searchsorted(pad_end, block_start, side='right'), N_EXPERTS - 1)
    xb = x[row_tok].reshape(n_blocks, MOE_BLOCK, d)

    def expert_block(args):
        x_blk, e = args
        hid = x_blk @ w1[e] + b1[e]
        glu, lin = hid[:, ::2], hid[:, 1::2]
        glu = jnp.minimum(glu, SWIGLU_LIMIT)
        lin = jnp.clip(lin, -SWIGLU_LIMIT, SWIGLU_LIMIT)
        act = glu * jax.nn.sigmoid(SWIGLU_ALPHA * glu) * (lin + 1.0)
        return act @ w2[e] + b2[e]

    yb = lax.map(expert_block, (xb, block_expert))
    y_rows = yb.reshape(n_rows, d) * row_gate[:, None].astype(yb.dtype)
    y = jax.ops.segment_sum(y_rows, row_tok, num_segments=n_tok)
    return y.reshape(bsz, s, d)


def setup_inputs(seed: int = 0) -> dict:
    key = jax.random.key(seed)
    ks = iter(jax.random.split(key, 32))
    f32 = jnp.float32

    def nrm(shape, scale):
        return jax.random.normal(next(ks), shape, f32) * scale

    def gain(shape):
        return 1.0 + 0.05 * jax.random.normal(next(ks), shape, f32)

    return {
        'x': nrm((BATCH, SEQ, D_MODEL), 1.0),
        'mix_norm': gain((DEPTH, D_MODEL)),
        'ffn_norm': gain((DEPTH, D_MODEL)),
        'even_w_in': nrm((N_EVEN, D_MODEL, EVEN_IN), D_MODEL ** -0.5),
        'even_w_s': nrm((N_EVEN, A_HEADS, A_CHUNK, A_CHUNK), A_CHUNK ** -0.5),
        'even_b_s': gain((N_EVEN, A_HEADS, A_CHUNK)),
        'even_g_v': gain((N_EVEN, A_HEADS, A_HEAD_DIM)),
        'even_g_q': gain((N_EVEN, B_HEAD_DIM)),
        'even_g_k': gain((N_EVEN, B_HEAD_DIM)),
        'even_w_out': nrm((N_EVEN, MIX_WIDTH, D_MODEL), 0.5 * MIX_WIDTH ** -0.5),
        'odd_w_in': nrm((N_ODD, D_MODEL, ODD_IN), D_MODEL ** -0.5),
        'odd_g_cq': gain((N_ODD, C_Q_LORA)),
        'odd_g_ckv': gain((N_ODD, C_KV_LORA)),
        'odd_w_uq': nrm((N_ODD, C_Q_LORA, C_HEADS * (C_NOPE + C_ROPE)), C_Q_LORA ** -0.5),
        'odd_w_ukv': nrm((N_ODD, C_KV_LORA, C_HEADS * (C_NOPE + C_V)), C_KV_LORA ** -0.5),
        'odd_g_q': gain((N_ODD, C_NOPE + C_ROPE)),
        'odd_g_k': gain((N_ODD, C_NOPE + C_ROPE)),
        'odd_w_out': nrm((N_ODD, C_HEADS * C_V, D_MODEL), 0.5 * (C_HEADS * C_V) ** -0.5),
        'router_w': nrm((DEPTH, D_MODEL, N_EXPERTS), D_MODEL ** -0.5),
        'router_b': nrm((DEPTH, N_EXPERTS), 0.01),
        'expert_w1': nrm((DEPTH, N_EXPERTS, D_MODEL, 2 * D_EXPERT), D_MODEL ** -0.5),
        'expert_b1': nrm((DEPTH, N_EXPERTS, 2 * D_EXPERT), 0.02),
        'expert_w2': nrm((DEPTH, N_EXPERTS, D_EXPERT, D_MODEL), 0.5 * D_EXPERT ** -0.5),
        'expert_b2': nrm((DEPTH, N_EXPERTS, D_MODEL), 0.02),
    }


def reference(x, mix_norm, ffn_norm, even_w_in, even_w_s, even_b_s, even_g_v, even_g_q, even_g_k, even_w_out, odd_w_in, odd_g_cq, odd_g_ckv, odd_w_uq, odd_w_ukv, odd_g_q, odd_g_k, odd_w_out, router_w, router_b, expert_w1, expert_b1, expert_w2, expert_b2):
    for l in range(DEPTH):
        i = l // 2
        h = rms_norm(x, mix_norm[l])
        if l % 2 == 0:
            x = x + even_mixer(h, even_w_in[i], even_w_s[i], even_b_s[i], even_g_v[i], even_g_q[i], even_g_k[i], even_w_out[i])
        else:
            x = x + mla_mixer(h, odd_w_in[i], odd_g_cq[i], odd_g_ckv[i], odd_w_uq[i], odd_w_ukv[i], odd_g_q[i], odd_g_k[i], odd_w_out[i])
        h = rms_norm(x, ffn_norm[l])
        x = x + moe_ffn(h, router_w[l], router_b[l], expert_w1[l], expert_b1[l], expert_w2[l], expert_b2[l])
    return x
```

```python
import functools
import math

import jax
import jax.numpy as jnp
from jax import lax
from jax.experimental import pallas as pl
from jax.experimental.pallas import tpu as pltpu

EPS = 1e-6
NEG_INF = -1e30
A_HEADS = 4
A_HEAD_DIM = 128
A_CHUNK = 128
A_WIDTH = A_HEADS * A_HEAD_DIM
B_HEADS = 8
B_HEAD_DIM = 64
B_WIDTH = B_HEADS * B_HEAD_DIM
B_BRANCHES = ((128, 1), (512, 4), (2048, 16))
B_BLOCK = 128
C_HEADS = 8
C_NOPE = 128
C_ROPE = 64
C_V = 128
C_Q_LORA = 512
C_KV_LORA = 256
C_QK_PAD = 256
ROPE_THETA = 10000.0
N_EXPERTS = 32
TOP_K = 4
D_EXPERT = 1024
SWIGLU_ALPHA = 1.702
SWIGLU_LIMIT = 7.0
MOE_BLOCK = 256

LANES = 128
TOKEN_TILE = 256
VMEM_LIMIT = 56 * 1024 * 1024

_F32 = jnp.float32
_BF16 = jnp.bfloat16


def _params(*sem):
    return pltpu.CompilerParams(dimension_semantics=sem, vmem_limit_bytes=VMEM_LIMIT)


def _rms(x, g):
    return x * lax.rsqrt(jnp.mean(x * x, axis=-1, keepdims=True) + EPS) * g


def _gelu(x):
    return 0.5 * x * (1.0 + lax.erf(x * math.sqrt(0.5)))


def _dot(a, b):
    return jnp.dot(a, b, preferred_element_type=_F32)


def _dot_nt(a, b):
    return lax.dot_general(a, b, (((1,), (1,)), ((), ())), preferred_element_type=_F32)


def _seg_sum(x2, seg):
    hi = x2.astype(_BF16)
    lo = (x2 - hi.astype(_F32)).astype(_BF16)
    return _dot(hi, seg) + _dot(lo, seg)


def _const_spec(shape):
    nd = len(shape)
    return pl.BlockSpec(shape, lambda *_: (0,) * nd)


def _even_in_body(x_ref, g_ref, win_ref, ws_ref, bst_ref, gv_ref, gq_ref, gk_ref, seg_ref,
                  a_ref, q_ref, k_ref, v_ref):
    x = x_ref[...]
    h = _rms(x, g_ref[...]).astype(_BF16)
    z = _dot(h, win_ref[...])
    tm = x.shape[0]
    u = _gelu(z[:, :A_WIDTH])
    vv = _gelu(z[:, A_WIDTH:2 * A_WIDTH])
    row = lax.broadcasted_iota(jnp.int32, (A_CHUNK, A_CHUNK), 0)
    col = lax.broadcasted_iota(jnp.int32, (A_CHUNK, A_CHUNK), 1)
    for hd in range(A_HEADS):
        sl = slice(hd * A_HEAD_DIM, (hd + 1) * A_HEAD_DIM)
        vh = _rms(vv[:, sl], gv_ref[hd:hd + 1, :]).astype(_BF16)
        w = jnp.where(row >= col, ws_ref[hd], 0.0).astype(_BF16)
        bias = bst_ref[:, hd:hd + 1]
        for c in range(tm // A_CHUNK):
            rs = slice(c * A_CHUNK, (c + 1) * A_CHUNK)
            mixed = _dot(w, vh[rs]) + bias
            a_ref[rs, sl] = (u[rs, sl] * mixed).astype(a_ref.dtype)
    o = 2 * A_WIDTH
    q = z[:, o:o + B_WIDTH]
    k = z[:, o + B_WIDTH:o + 2 * B_WIDTH]
    seg = seg_ref[...]
    inv = 1.0 / B_HEAD_DIM
    qn = q * lax.rsqrt(_seg_sum(q * q, seg) * inv + EPS) * gq_ref[...]
    kn = k * lax.rsqrt(_seg_sum(k * k, seg) * inv + EPS) * gk_ref[...]
    q_ref[...] = (qn * (B_HEAD_DIM ** -0.5)).astype(q_ref.dtype)
    k_ref[...] = kn.astype(k_ref.dtype)
    v_ref[...] = z[:, o + 2 * B_WIDTH:].astype(v_ref.dtype)


def _even_in(x, g, w_in, w_s, b_s, g_v, g_q, g_k):
    t, d = x.shape
    tm = TOKEN_TILE
    seg = (jnp.arange(B_WIDTH)[:, None] // B_HEAD_DIM == jnp.arange(B_WIDTH)[None, :] // B_HEAD_DIM).astype(_BF16)
    out = jax.ShapeDtypeStruct((t, B_WIDTH), _BF16)
    tile = lambda w: pl.BlockSpec((tm, w), lambda i: (i, 0))
    return pl.pallas_call(
        _even_in_body,
        grid=(t // tm,),
        in_specs=[tile(d), _const_spec((1, d)), _const_spec(w_in.shape), _const_spec(w_s.shape),
                  _const_spec((A_CHUNK, A_HEADS)), _const_spec(g_v.shape), _const_spec((1, B_WIDTH)),
                  _const_spec((1, B_WIDTH)), _const_spec(seg.shape)],
        out_specs=[tile(A_WIDTH), tile(B_WIDTH), tile(B_WIDTH), tile(B_WIDTH)],
        out_shape=[jax.ShapeDtypeStruct((t, A_WIDTH), _BF16), out, out, out],
        compiler_params=_params("parallel"),
        name="even_in",
    )(x, g.reshape(1, d), w_in.astype(_BF16), w_s, b_s.T, g_v,
      jnp.tile(g_q, B_HEADS).reshape(1, B_WIDTH), jnp.tile(g_k, B_HEADS).reshape(1, B_WIDTH), seg)


def _dilated_body(q_ref, kp_ref, kc_ref, vp_ref, vc_ref, o_ref, lse_ref):
    jb = pl.program_id(2)
    q = q_ref[0]
    kk = jnp.concatenate([kp_ref[0], kc_ref[0]], axis=0)
    vv = jnp.concatenate([vp_ref[0], vc_ref[0]], axis=0)
    i = lax.broadcasted_iota(jnp.int32, (B_BLOCK, 2 * B_BLOCK), 0)
    c = lax.broadcasted_iota(jnp.int32, (B_BLOCK, 2 * B_BLOCK), 1)
    valid = (c >= i) & (c <= i + B_BLOCK) & ((jb > 0) | (c >= B_BLOCK))
    lane = lax.broadcasted_iota(jnp.int32, (1, LANES), 1)
    first = lane < B_HEAD_DIM
    for p in range(B_WIDTH // LANES):
        sl = slice(p * LANES, (p + 1) * LANES)
        q2, k2, v2 = q[:, sl], kk[:, sl], vv[:, sl]
        outs, lses = [], []
        for half in (first, jnp.logical_not(first)):
            s = _dot_nt(jnp.where(half, q2, jnp.zeros_like(q2)), k2)
            s = jnp.where(valid, s, NEG_INF)
            m = jnp.max(s, axis=-1, keepdims=True)
            e = jnp.exp(s - m)
            l = jnp.sum(e, axis=-1, keepdims=True)
            outs.append(_dot(e.astype(_BF16), v2) / l)
            lses.append(m + jnp.log(l))
        o_ref[0, :, sl] = jnp.where(first, outs[0], outs[1])
        lse_ref[0, :, sl] = jnp.where(first, lses[0], lses[1])


def _dilated_branch(q, k, v, bsz, dil):
    t = q.shape[0]
    s = t // bsz
    n = s // dil
    view = lambda a: a.reshape(bsz, n, dil * B_WIDTH)
    cur = pl.BlockSpec((1, B_BLOCK, B_WIDTH), lambda b, r, j: (b, j, r))
    prev = pl.BlockSpec((1, B_BLOCK, B_WIDTH), lambda b, r, j: (b, jnp.maximum(j - 1, 0), r))
    shape = jax.ShapeDtypeStruct((bsz, n, dil * B_WIDTH), _F32)
    o, lse = pl.pallas_call(
        _dilated_body,
        grid=(bsz, dil, n // B_BLOCK),
        in_specs=[cur, prev, cur, prev, cur],
        out_specs=[cur, cur],
        out_shape=[shape, shape],
        compiler_params=_params("parallel", "parallel", "parallel"),
        name=f"dilated_d{dil}",
    )(view(q), view(k), view(k), view(v), view(v))
    return o.reshape(t, B_WIDTH), lse.reshape(t, B_WIDTH)


def _even_out_body(x_ref, a_ref, o1, o2, o3, l1, l2, l3, w_ref, out_ref):
    la, lb, lc = l1[...], l2[...], l3[...]
    m = jnp.maximum(jnp.maximum(la, lb), lc)
    ea, eb, ec = jnp.exp(la - m), jnp.exp(lb - m), jnp.exp(lc - m)
    b = (ea * o1[...] + eb * o2[...] + ec * o3[...]) / (ea + eb + ec)
    y = _dot(a_ref[...], w_ref[:A_WIDTH, :]) + _dot(b.astype(_BF16), w_ref[A_WIDTH:, :])
    out_ref[...] = x_ref[...] + y


def _even_out(x, a, outs, lses, w_out):
    t, d = x.shape
    tm = TOKEN_TILE
    tile = lambda w: pl.BlockSpec((tm, w), lambda i: (i, 0))
    return pl.pallas_call(
        _even_out_body,
        grid=(t // tm,),
        in_specs=[tile(d), tile(A_WIDTH)] + [tile(B_WIDTH)] * 6 + [_const_spec(w_out.shape)],
        out_specs=tile(d),
        out_shape=jax.ShapeDtypeStruct((t, d), _F32),
        compiler_params=_params("parallel"),
        name="even_out",
    )(x, a, *outs, *lses, w_out.astype(_BF16))


def _rope(x, cos, sin_signed, low):
    partner = jnp.where(low, pltpu.roll(x, LANES - C_ROPE // 2, 1), pltpu.roll(x, C_ROPE // 2, 1))
    return x * cos + partner * sin_signed


def _odd_in_body(x_ref, g_ref, win_ref, gcq_ref, gckv_ref, wqn_ref, wqr_ref, wkn_ref, wv_ref,
                 gqn_ref, gqr_ref, gkn_ref, gkr_ref, cos_ref, sin_ref, q_ref, k_ref, v_ref):
    x = x_ref[0]
    h = _rms(x, g_ref[...]).astype(_BF16)
    z = _dot(h, win_ref[...])
    cq = _rms(z[:, :C_Q_LORA], gcq_ref[...]).astype(_BF16)
    ckv = _rms(z[:, C_Q_LORA:C_Q_LORA + C_KV_LORA], gckv_ref[...]).astype(_BF16)
    kr = z[:, C_Q_LORA + C_KV_LORA:]
    qn = _dot(cq, wqn_ref[...])
    qr = _dot(cq, wqr_ref[...])
    kn = _dot(ckv, wkn_ref[...])
    vv = _dot(ckv, wv_ref[...])
    cos, sin = cos_ref[...], sin_ref[...]
    lane = lax.broadcasted_iota(jnp.int32, (1, LANES), 1)
    low = (lane % C_ROPE) < (C_ROPE // 2)
    inv = 1.0 / (C_NOPE + C_ROPE)
    scale = (C_NOPE + C_ROPE) ** -0.5
    kr_ss = jnp.sum(kr * kr, axis=-1, keepdims=True)
    kr_roped = _rope(kr * gkr_ref[...], cos, sin, low)
    for hd in range(C_HEADS):
        sl = slice(hd * LANES, (hd + 1) * LANES)
        qnh, qrh, knh = qn[:, sl], qr[:, sl], kn[:, sl]
        ssq = jnp.sum(qnh * qnh, axis=-1, keepdims=True) + jnp.sum(qrh * qrh, axis=-1, keepdims=True)
        rq = lax.rsqrt(ssq * inv + EPS) * scale
        q_ref[0, hd, :, :LANES] = (qnh * rq * gqn_ref[...]).astype(q_ref.dtype)
        q_ref[0, hd, :, LANES:] = (_rope(qrh * gqr_ref[...], cos, sin, low) * rq).astype(q_ref.dtype)
        ssk = jnp.sum(knh * knh, axis=-1, keepdims=True) + kr_ss
        rk = lax.rsqrt(ssk * inv + EPS)
        k_ref[0, hd, :, :LANES] = (knh * rk * gkn_ref[...]).astype(k_ref.dtype)
        k_ref[0, hd, :, LANES:] = (kr_roped * rk).astype(k_ref.dtype)
        v_ref[0, hd] = vv[:, sl].astype(v_ref.dtype)


def _pad_lanes(a, width):
    return jnp.pad(a, [(0, 0)] * (a.ndim - 1) + [(0, width - a.shape[-1])])


def _odd_in(x, g, w_in, g_cq, g_ckv, w_uq, w_ukv, g_q, g_k, bsz):
    t, d = x.shape
    s = t // bsz
    tm = TOKEN_TILE
    w_in_p = _pad_lanes(w_in, C_Q_LORA + C_KV_LORA + LANES).astype(_BF16)
    wq = w_uq.reshape(C_Q_LORA, C_HEADS, C_NOPE + C_ROPE)
    wqn = wq[:, :, :C_NOPE].reshape(C_Q_LORA, C_HEADS * C_NOPE).astype(_BF16)
    wqr = _pad_lanes(wq[:, :, C_NOPE:], LANES).reshape(C_Q_LORA, C_HEADS * LANES).astype(_BF16)
    wkv = w_ukv.reshape(C_KV_LORA, C_HEADS, C_NOPE + C_V)
    wkn = wkv[:, :, :C_NOPE].reshape(C_KV_LORA, C_HEADS * C_NOPE).astype(_BF16)
    wv = wkv[:, :, C_NOPE:].reshape(C_KV_LORA, C_HEADS * C_V).astype(_BF16)
    half = C_ROPE // 2
    inv_freq = ROPE_THETA ** (-jnp.arange(half, dtype=_F32) / half)
    ang = jnp.arange(s, dtype=_F32)[:, None] * inv_freq[None, :]
    cos = jnp.tile(jnp.cos(ang), (1, LANES // half))
    sin = jnp.tile(jnp.concatenate([-jnp.sin(ang), jnp.sin(ang)], axis=-1), (1, LANES // C_ROPE))
    row = lambda a: a.reshape(1, -1)
    qk_shape = jax.ShapeDtypeStruct((bsz, C_HEADS, s, C_QK_PAD), _BF16)
    head_spec = lambda w: pl.BlockSpec((1, C_HEADS, tm, w), lambda b, j: (b, 0, j, 0))
    pos_spec = pl.BlockSpec((tm, LANES), lambda b, j: (j, 0))
    consts = [row(g), w_in_p, row(g_cq), row(g_ckv), wqn, wqr, wkn, wv,
              row(g_q[:C_NOPE]), row(_pad_lanes(g_q[C_NOPE:], LANES)),
              row(g_k[:C_NOPE]), row(_pad_lanes(g_k[C_NOPE:], LANES))]
    return pl.pallas_call(
        _odd_in_body,
        grid=(bsz, s // tm),
        in_specs=[pl.BlockSpec((1, tm, d), lambda b, j: (b, j, 0))]
        + [_const_spec(c.shape) for c in consts] + [pos_spec, pos_spec],
        out_specs=[head_spec(C_QK_PAD), head_spec(C_QK_PAD), head_spec(C_V)],
        out_shape=[qk_shape, qk_shape, jax.ShapeDtypeStruct((bsz, C_HEADS, s, C_V), _BF16)],
        compiler_params=_params("parallel", "parallel"),
        name="odd_in",
    )(x.reshape(bsz, s, d), *consts, cos, sin)


def _flash_body(q_ref, k_ref, v_ref, o_ref, m_ref, l_ref, acc_ref):
    qi, ki = pl.program_id(2), pl.program_id(3)

    @pl.when(ki == 0)
    def _():
        m_ref[...] = jnp.full_like(m_ref, NEG_INF)
        l_ref[...] = jnp.zeros_like(l_ref)
        acc_ref[...] = jnp.zeros_like(acc_ref)

    def step(masked):
        s = _dot_nt(q_ref[0, 0], k_ref[0, 0])
        if masked:
            r = lax.broadcasted_iota(jnp.int32, s.shape, 0)
            c = lax.broadcasted_iota(jnp.int32, s.shape, 1)
            s = jnp.where(c <= r, s, NEG_INF)
        m_old = m_ref[...]
        m_new = jnp.maximum(m_old, jnp.max(s, axis=-1, keepdims=True))
        alpha = jnp.exp(m_old - m_new)
        p = jnp.exp(s - m_new)
        l_ref[...] = alpha * l_ref[...] + jnp.sum(p, axis=-1, keepdims=True)
        acc_ref[...] = alpha * acc_ref[...] + _dot(p.astype(_BF16), v_ref[0, 0])
        m_ref[...] = m_new

    @pl.when(ki < qi)
    def _():
        step(False)

    @pl.when(ki == qi)
    def _():
        step(True)
        o_ref[0] = (acc_ref[...] / l_ref[...]).astype(o_ref.dtype)


def _flash(q, k, v, tq):
    bsz, nh, s, dq = q.shape
    dv = v.shape[-1]
    nq = s // tq
    kv_idx = lambda b, h, i, j: (b, h, jnp.minimum(i, j), 0)
    return pl.pallas_call(
        _flash_body,
        grid=(bsz, nh, nq, nq),
        in_specs=[pl.BlockSpec((1, 1, tq, dq), lambda b, h, i, j: (b, h, i, 0)),
                  pl.BlockSpec((1, 1, tq, dq), kv_idx), pl.BlockSpec((1, 1, tq, dv), kv_idx)],
        out_specs=pl.BlockSpec((1, tq, dv), lambda b, h, i, j: (b, i, h)),
        out_shape=jax.ShapeDtypeStruct((bsz, s, nh * dv), _BF16),
        scratch_shapes=[pltpu.VMEM((tq, 1), _F32), pltpu.VMEM((tq, 1), _F32), pltpu.VMEM((tq, dv), _F32)],
        compiler_params=_params("parallel", "parallel", "parallel", "arbitrary"),
        name="mla_flash",
    )(q, k, v)


def _proj_out_body(x_ref, o_ref, w_ref, out_ref):
    out_ref[...] = x_ref[...] + _dot(o_ref[...], w_ref[...])


def _proj_out(x, o, w_out):
    t, d = x.shape
    tm = TOKEN_TILE
    tile = lambda w: pl.BlockSpec((tm, w), lambda i: (i, 0))
    return pl.pallas_call(
        _proj_out_body,
        grid=(t // tm,),
        in_specs=[tile(d), tile(o.shape[1]), _const_spec(w_out.shape)],
        out_specs=tile(d),
        out_shape=jax.ShapeDtypeStruct((t, d), _F32),
        compiler_params=_params("parallel"),
        name="odd_out",
    )(x, o, w_out.astype(_BF16))


def _router_body(x_ref, g_ref, w_ref, b_ref, hp_ref, idx_ref, gate_ref, cnt_ref, base_ref):
    @pl.when(pl.program_id(0) == 0)
    def _():
        base_ref[...] = jnp.zeros_like(base_ref)

    h = _rms(x_ref[...], g_ref[...])
    tm, d = h.shape
    logits = jnp.dot(h, w_ref[...], preferred_element_type=_F32, precision=lax.Precision.HIGHEST) + b_ref[...]
    hb = h.astype(_BF16).astype(_F32)
    lo = lax.bitcast_convert_type(hb[:, :d // 2], jnp.uint32)
    hi = lax.bitcast_convert_type(hb[:, d // 2:], jnp.uint32)
    hp_ref[...] = (lo >> 16) | hi

    col = lax.broadcasted_iota(jnp.int32, logits.shape, 1)
    lane = lax.broadcasted_iota(jnp.int32, (tm, LANES), 1)
    r = lax.broadcasted_iota(jnp.int32, (tm, tm), 0)
    c = lax.broadcasted_iota(jnp.int32, (tm, tm), 1)
    before = (c < r).astype(_BF16)
    base = base_ref[...]
    idx_out = jnp.zeros((tm, LANES), jnp.int32)
    gate_out = jnp.zeros((tm, LANES), _F32)
    top0 = None
    denom = None
    work = logits
    for k in range(TOP_K):
        m = jnp.max(work, axis=-1, keepdims=True)
        sel = jnp.min(jnp.where(work == m, col, N_EXPERTS), axis=-1, keepdims=True)
        hit = col == sel
        work = jnp.where(hit, -jnp.inf, work)
        onehot = hit.astype(_F32)
        earlier = _dot(before, hit.astype(_BF16)) + base
        rank = jnp.sum(onehot * earlier, axis=-1, keepdims=True).astype(jnp.int32)
        base = base + jnp.sum(onehot, axis=0, keepdims=True)
        if k == 0:
            top0 = m
            e = jnp.ones_like(m)
            denom = e
        else:
            e = jnp.exp(m - top0)
            denom = denom + e
        idx_out = jnp.where(lane == k, sel, idx_out)
        idx_out = jnp.where(lane == TOP_K + k, rank, idx_out)
        gate_out = jnp.where(lane == k, e, gate_out)
    base_ref[...] = base
    cnt_ref[...] = base
    idx_ref[...] = idx_out
    gate_ref[...] = gate_out / denom


def _router(x, g, w_r, b_r):
    t, d = x.shape
    tm = TOKEN_TILE
    tile = lambda w: pl.BlockSpec((tm, w), lambda i: (i, 0))
    return pl.pallas_call(
        _router_body,
        grid=(t // tm,),
        in_specs=[tile(d), _const_spec((1, d)), _const_spec(w_r.shape), _const_spec((1, N_EXPERTS))],
        out_specs=[tile(d // 2), tile(LANES), tile(LANES), _const_spec((1, N_EXPERTS))],
        out_shape=[jax.ShapeDtypeStruct((t, d // 2), jnp.uint32), jax.ShapeDtypeStruct((t, LANES), jnp.int32),
                   jax.ShapeDtypeStruct((t, LANES), _F32), jax.ShapeDtypeStruct((1, N_EXPERTS), _F32)],
        scratch_shapes=[pltpu.VMEM((1, N_EXPERTS), _F32)],
        compiler_params=_params("arbitrary"),
        name="moe_router",
    )(x, g.reshape(1, d), w_r, b_r.reshape(1, N_EXPERTS))


def _dispatch_body(slot_ref, hp_ref, xb_in_ref, xb_ref, sem):
    del xb_in_ref
    tm = hp_ref.shape[0]

    def row_copy(r, s):
        return pltpu.make_async_copy(hp_ref.at[pl.ds(r, 1)], xb_ref.at[pl.ds(s, 1)], sem)

    def start(r, carry):
        for k in range(TOP_K):
            row_copy(r, slot_ref[r * TOP_K + k]).start()
        return carry

    def wait(r, carry):
        for k in range(TOP_K):
            row_copy(r, slot_ref[r * TOP_K + k]).wait()
        return carry

    lax.fori_loop(0, tm, start, 0)
    lax.fori_loop(0, tm, wait, 0)


def _dispatch(hp, slot_flat, n_rows):
    t, w = hp.shape
    tm = TOKEN_TILE
    return pl.pallas_call(
        _dispatch_body,
        grid=(t // tm,),
        in_specs=[pl.BlockSpec((tm * TOP_K,), lambda i: (i,), memory_space=pltpu.SMEM),
                  pl.BlockSpec((tm, w), lambda i: (i, 0)),
                  pl.BlockSpec(memory_space=pl.ANY)],
        out_specs=pl.BlockSpec(memory_space=pl.ANY),
        out_shape=jax.ShapeDtypeStruct((n_rows, w), hp.dtype),
        scratch_shapes=[pltpu.SemaphoreType.DMA(())],
        input_output_aliases={2: 0},
        compiler_params=_params("arbitrary"),
        name="moe_dispatch",
    )(slot_flat, hp, jnp.zeros((n_rows, w), hp.dtype))


def _expert_body(be_ref, nu_ref, xb_ref, w1_ref, b1g_ref, b1l_ref, w2_ref, b2_ref, perm_ref,
                 yb_ref, w1g_s, w1l_s, w2_s):
    b = pl.program_id(0)
    live = b < nu_ref[0]
    changed = (b == 0) | (be_ref[b] != be_ref[jnp.maximum(b - 1, 0)])

    @pl.when(live & changed)
    def _():
        group = perm_ref.shape[0]
        for c in range(w1_ref.shape[2] // group):
            wc = w1_ref[0, :, c * group:(c + 1) * group].astype(_BF16)
            d = _dot(wc, perm_ref[...]).astype(_BF16)
            w1g_s[:, c * (group // 2):(c + 1) * (group // 2)] = d[:, :group // 2]
            w1l_s[:, c * (group // 2):(c + 1) * (group // 2)] = d[:, group // 2:]
        w2_s[...] = w2_ref[0].astype(_BF16)

    @pl.when(live)
    def _():
        words = xb_ref[...]
        lo = lax.bitcast_convert_type(words << 16, _F32).astype(_BF16)
        hi = lax.bitcast_convert_type(words & jnp.uint32(0xFFFF0000), _F32).astype(_BF16)
        x = jnp.concatenate([lo, hi], axis=1)
        glu = _dot(x, w1g_s[...]) + b1g_ref[0]
        lin = _dot(x, w1l_s[...]) + b1l_ref[0]
        glu = jnp.minimum(glu, SWIGLU_LIMIT)
        lin = jnp.clip(lin, -SWIGLU_LIMIT, SWIGLU_LIMIT)
        act = glu * jax.nn.sigmoid(SWIGLU_ALPHA * glu) * (lin + 1.0)
        yb_ref[...] = _dot(act.astype(_BF16), w2_s[...]) + b2_ref[0]

    @pl.when(jnp.logical_not(live))
    def _():
        yb_ref[...] = jnp.zeros_like(yb_ref)


def _experts(xb, block_expert, n_used, w1, b1, w2, b2):
    n_rows, half_d = xb.shape
    d = 2 * half_d
    n_blocks = n_rows // MOE_BLOCK
    n_e, _, two_f = w1.shape
    f = two_f // 2
    group = 2 * LANES
    j = jnp.arange(group)
    src = jnp.where(j < LANES, 2 * j, 2 * (j - LANES) + 1)
    perm = (jnp.arange(group)[:, None] == src[None, :]).astype(_BF16)
    b1p = b1.reshape(n_e, 1, f, 2)
    blk = lambda b, be, nu: jnp.minimum(b, nu[0] - 1)
    e_spec = lambda shape: pl.BlockSpec((1,) + shape, lambda b, be, nu: (be[b],) + (0,) * len(shape))
    grid_spec = pltpu.PrefetchScalarGridSpec(
        num_scalar_prefetch=2,
        grid=(n_blocks,),
        in_specs=[pl.BlockSpec((MOE_BLOCK, half_d), lambda b, be, nu: (blk(b, be, nu), 0)),
                  e_spec((d, two_f)), e_spec((1, f)), e_spec((1, f)), e_spec((f, d)), e_spec((1, d)),
                  pl.BlockSpec((group, group), lambda b, be, nu: (0, 0))],
        out_specs=pl.BlockSpec((MOE_BLOCK, d), lambda b, be, nu: (b, 0)),
        scratch_shapes=[pltpu.VMEM((d, f), _BF16), pltpu.VMEM((d, f), _BF16), pltpu.VMEM((f, d), _BF16)],
    )
    return pl.pallas_call(
        _expert_body,
        grid_spec=grid_spec,
        out_shape=jax.ShapeDtypeStruct((n_rows, d), _F32),
        compiler_params=_params("arbitrary"),
        name="moe_experts",
    )(block_expert, n_used, xb, w1, b1p[..., 0], b1p[..., 1], w2, b2.reshape(n_e, 1, d), perm)


def _combine_body(slot_ref, x_ref, gate_ref, yb_ref, out_ref, rows_ref, sem):
    tm = x_ref.shape[0]

    def row_copy(r, k):
        s = slot_ref[r * TOP_K + k]
        return pltpu.make_async_copy(yb_ref.at[pl.ds(s, 1)], rows_ref.at[k, pl.ds(r, 1)], sem)

    def start(r, carry):
        for k in range(TOP_K):
            row_copy(r, k).start()
        return carry

    def wait(r, carry):
        for k in range(TOP_K):
            row_copy(r, k).wait()
        return carry

    lax.fori_loop(0, tm, start, 0)
    lax.fori_loop(0, tm, wait, 0)
    acc = x_ref[...]
    for k in range(TOP_K):
        acc = acc + gate_ref[:, k:k + 1] * rows_ref[k]
    out_ref[...] = acc


def _combine(x, gates, slot_flat, yb):
    t, d = x.shape
    tm = TOKEN_TILE
    return pl.pallas_call(
        _combine_body,
        grid=(t // tm,),
        in_specs=[pl.BlockSpec((tm * TOP_K,), lambda i: (i,), memory_space=pltpu.SMEM),
                  pl.BlockSpec((tm, d), lambda i: (i, 0)),
                  pl.BlockSpec((tm, LANES), lambda i: (i, 0)),
                  pl.BlockSpec(memory_space=pl.ANY)],
        out_specs=pl.BlockSpec((tm, d), lambda i: (i, 0)),
        out_shape=jax.ShapeDtypeStruct((t, d), _F32),
        scratch_shapes=[pltpu.VMEM((TOP_K, tm, d), _F32), pltpu.SemaphoreType.DMA(())],
        compiler_params=_params("arbitrary"),
        name="moe_combine",
    )(slot_flat, x, gates, yb)


def _moe(x, g, w_r, b_r, w1, b1, w2, b2):
    t = x.shape[0]
    n_blocks = -(-(t * TOP_K + N_EXPERTS * (MOE_BLOCK - 1)) // MOE_BLOCK)
    hp, idx, gates, counts = _router(x, g, w_r, b_r)
    counts = counts[0].astype(jnp.int32)
    padded = (counts + MOE_BLOCK - 1) // MOE_BLOCK * MOE_BLOCK
    pad_end = jnp.cumsum(padded)
    pad_start = pad_end - padded
    slot = pad_start[idx[:, :TOP_K]] + idx[:, TOP_K:2 * TOP_K]
    slot_flat = slot.reshape(-1).astype(jnp.int32)
    n_used = (pad_end[-1] // MOE_BLOCK).astype(jnp.int32)
    block_id = jnp.minimum(jnp.arange(n_blocks, dtype=jnp.int32), n_used - 1)
    block_expert = jnp.searchsorted(pad_end, block_id * MOE_BLOCK, side='right').astype(jnp.int32)
    xb = _dispatch(hp, slot_flat, n_blocks * MOE_BLOCK)
    yb = _experts(xb, block_expert, n_used.reshape(1), w1, b1, w2, b2)
    return _combine(x, gates, slot_flat, yb)


def kernel(x, mix_norm, ffn_norm, even_w_in, even_w_s, even_b_s, even_g_v, even_g_q, even_g_k, even_w_out, odd_w_in, odd_g_cq, odd_g_ckv, odd_w_uq, odd_w_ukv, odd_g_q, odd_g_k, odd_w_out, router_w, router_b, expert_w1, expert_b1, expert_w2, expert_b2):
    bsz, s, d = x.shape
    xt = x.reshape(bsz * s, d)
    for l in range(mix_norm.shape[0]):
        i = l // 2
        if l % 2 == 0:
            a, q, k, v = _even_in(xt, mix_norm[l], even_w_in[i], even_w_s[i], even_b_s[i], even_g_v[i],
                                  even_g_q[i], even_g_k[i])
            branches = [_dilated_branch(q, k, v, bsz, dil) for _, dil in B_BRANCHES]
            xt = _even_out(xt, a, [o for o, _ in branches], [lse for _, lse in branches], even_w_out[i])
        else:
            q, k, v = _odd_in(xt, mix_norm[l], odd_w_in[i], odd_g_cq[i], odd_g_ckv[i], odd_w_uq[i],
                              odd_w_ukv[i], odd_g_q[i], odd_g_k[i], bsz)
            o = _flash(q, k, v, 512)
            xt = _proj_out(xt, o.reshape(bsz * s, -1), odd_w_out[i])
        xt = _moe(xt, ffn_norm[l], router_w[l], router_b[l], expert_w1[l], expert_b1[l],
                  expert_w2[l], expert_b2[l])
    return xt.reshape(bsz, s, d)
```

```python
import functools
import math

import jax
import jax.numpy as jnp
from jax import lax
from jax.experimental import pallas as pl
from jax.experimental.pallas import tpu as pltpu

EPS = 1e-6
NEG_INF = -1e30
A_HEADS = 4
A_HEAD_DIM = 128
A_CHUNK = 128
A_WIDTH = A_HEADS * A_HEAD_DIM
B_HEADS = 8
B_HEAD_DIM = 64
B_WIDTH = B_HEADS * B_HEAD_DIM
B_BRANCHES = ((128, 1), (512, 4), (2048, 16))
B_BLOCK = 128
C_HEADS = 8
C_NOPE = 128
C_ROPE = 64
C_V = 128
C_Q_LORA = 512
C_KV_LORA = 256
C_QK_PAD = 256
ROPE_THETA = 10000.0
N_EXPERTS = 32
TOP_K = 4
D_EXPERT = 1024
SWIGLU_ALPHA = 1.702
SWIGLU_LIMIT = 7.0
MOE_BLOCK = 256

LANES = 128
TOKEN_TILE = 256
VMEM_LIMIT = 56 * 1024 * 1024

_F32 = jnp.float32
_BF16 = jnp.bfloat16


def _params(*sem):
    return pltpu.CompilerParams(dimension_semantics=sem, vmem_limit_bytes=VMEM_LIMIT)


def _rms(x, g):
    return x * lax.rsqrt(jnp.mean(x * x, axis=-1, keepdims=True) + EPS) * g


def _gelu(x):
    return 0.5 * x * (1.0 + lax.erf(x * math.sqrt(0.5)))


def _dot(a, b):
    return jnp.dot(a, b, preferred_element_type=_F32)


def _dot_nt(a, b):
    return lax.dot_general(a, b, (((1,), (1,)), ((), ())), preferred_element_type=_F32)


def _seg_sum(x2, seg):
    hi = x2.astype(_BF16)
    lo = (x2 - hi.astype(_F32)).astype(_BF16)
    return _dot(hi, seg) + _dot(lo, seg)


def _const_spec(shape):
    nd = len(shape)
    return pl.BlockSpec(shape, lambda *_: (0,) * nd)


def _even_in_body(x_ref, g_ref, win_ref, ws_ref, bst_ref, gv_ref, gq_ref, gk_ref, seg_ref,
                  a_ref, q_ref, k_ref, v_ref):
    x = x_ref[...]
    h = _rms(x, g_ref[...]).astype(_BF16)
    z = _dot(h, win_ref[...])
    tm = x.shape[0]
    u = _gelu(z[:, :A_WIDTH])
    vv = _gelu(z[:, A_WIDTH:2 * A_WIDTH])
    row = lax.broadcasted_iota(jnp.int32, (A_CHUNK, A_CHUNK), 0)
    col = lax.broadcasted_iota(jnp.int32, (A_CHUNK, A_CHUNK), 1)
    for hd in range(A_HEADS):
        sl = slice(hd * A_HEAD_DIM, (hd + 1) * A_HEAD_DIM)
        vh = _rms(vv[:, sl], gv_ref[hd:hd + 1, :]).astype(_BF16)
        w = jnp.where(row >= col, ws_ref[hd], 0.0).astype(_BF16)
        bias = bst_ref[:, hd:hd + 1]
        for c in range(tm // A_CHUNK):
            rs = slice(c * A_CHUNK, (c + 1) * A_CHUNK)
            mixed = _dot(w, vh[rs]) + bias
            a_ref[rs, sl] = (u[rs, sl] * mixed).astype(a_ref.dtype)
    o = 2 * A_WIDTH
    q = z[:, o:o + B_WIDTH]
    k = z[:, o + B_WIDTH:o + 2 * B_WIDTH]
    seg = seg_ref[...]
    inv = 1.0 / B_HEAD_DIM
    qn = q * lax.rsqrt(_seg_sum(q * q, seg) * inv + EPS) * gq_ref[...]
    kn = k * lax.rsqrt(_seg_sum(k * k, seg) * inv + EPS) * gk_ref[...]
    q_ref[...] = (qn * (B_HEAD_DIM ** -0.5)).astype(q_ref.dtype)
    k_ref[...] = kn.astype(k_ref.dtype)
    v_ref[...] = z[:, o + 2 * B_WIDTH:].astype(v_ref.dtype)


def _even_in(x, g, w_in, w_s, b_s, g_v, g_q, g_k):
    t, d = x.shape
    tm = TOKEN_TILE
    seg = (jnp.arange(B_WIDTH)[:, None] // B_HEAD_DIM == jnp.arange(B_WIDTH)[None, :] // B_HEAD_DIM).astype(_BF16)
    out = jax.ShapeDtypeStruct((t, B_WIDTH), _BF16)
    tile = lambda w: pl.BlockSpec((tm, w), lambda i: (i, 0))
    return pl.pallas_call(
        _even_in_body,
        grid=(t // tm,),
        in_specs=[tile(d), _const_spec((1, d)), _const_spec(w_in.shape), _const_spec(w_s.shape),
                  _const_spec((A_CHUNK, A_HEADS)), _const_spec(g_v.shape), _const_spec((1, B_WIDTH)),
                  _const_spec((1, B_WIDTH)), _const_spec(seg.shape)],
        out_specs=[tile(A_WIDTH), tile(B_WIDTH), tile(B_WIDTH), tile(B_WIDTH)],
        out_shape=[jax.ShapeDtypeStruct((t, A_WIDTH), _BF16), out, out, out],
        compiler_params=_params("parallel"),
        name="even_in",
    )(x, g.reshape(1, d), w_in.astype(_BF16), w_s, b_s.T, g_v,
      jnp.tile(g_q, B_HEADS).reshape(1, B_WIDTH), jnp.tile(g_k, B_HEADS).reshape(1, B_WIDTH), seg)


def _dilated_body(q_ref, kp_ref, kc_ref, vp_ref, vc_ref, o_ref, lse_ref):
    jb = pl.program_id(2)
    q = q_ref[0]
    kk = jnp.concatenate([kp_ref[0], kc_ref[0]], axis=0)
    vv = jnp.concatenate([vp_ref[0], vc_ref[0]], axis=0)
    i = lax.broadcasted_iota(jnp.int32, (B_BLOCK, 2 * B_BLOCK), 0)
    c = lax.broadcasted_iota(jnp.int32, (B_BLOCK, 2 * B_BLOCK), 1)
    valid = (c >= i) & (c <= i + B_BLOCK) & ((jb > 0) | (c >= B_BLOCK))
    lane = lax.broadcasted_iota(jnp.int32, (1, LANES), 1)
    first = lane < B_HEAD_DIM
    for p in range(B_WIDTH // LANES):
        sl = slice(p * LANES, (p + 1) * LANES)
        q2, k2, v2 = q[:, sl], kk[:, sl], vv[:, sl]
        outs, lses = [], []
        for half in (first, jnp.logical_not(first)):
            s = _dot_nt(jnp.where(half, q2, jnp.zeros_like(q2)), k2)
            s = jnp.where(valid, s, NEG_INF)
            m = jnp.max(s, axis=-1, keepdims=True)
            e = jnp.exp(s - m)
            l = jnp.sum(e, axis=-1, keepdims=True)
            outs.append(_dot(e.astype(_BF16), v2) / l)
            lses.append(m + jnp.log(l))
        o_ref[0, :, sl] = jnp.where(first, outs[0], outs[1])
        lse_ref[0, :, sl] = jnp.where(first, lses[0], lses[1])


def _dilated_branch(q, k, v, bsz, dil):
    t = q.shape[0]
    s = t // bsz
    n = s // dil
    view = lambda a: a.reshape(bsz, n, dil * B_WIDTH)
    cur = pl.BlockSpec((1, B_BLOCK, B_WIDTH), lambda b, r, j: (b, j, r))
    prev = pl.BlockSpec((1, B_BLOCK, B_WIDTH), lambda b, r, j: (b, jnp.maximum(j - 1, 0), r))
    shape = jax.ShapeDtypeStruct((bsz, n, dil * B_WIDTH), _F32)
    o, lse = pl.pallas_call(
        _dilated_body,
        grid=(bsz, dil, n // B_BLOCK),
        in_specs=[cur, prev, cur, prev, cur],
        out_specs=[cur, cur],
        out_shape=[shape, shape],
        compiler_params=_params("parallel", "parallel", "parallel"),
        name=f"dilated_d{dil}",
    )(view(q), view(k), view(k), view(v), view(v))
    return o.reshape(t, B_WIDTH), lse.reshape(t, B_WIDTH)


def _even_out_body(x_ref, a_ref, o1, o2, o3, l1, l2, l3, w_ref, out_ref):
    la, lb, lc = l1[...], l2[...], l3[...]
    m = jnp.maximum(jnp.maximum(la, lb), lc)
    ea, eb, ec = jnp.exp(la - m), jnp.exp(lb - m), jnp.exp(lc - m)
    b = (ea * o1[...] + eb * o2[...] + ec * o3[...]) / (ea + eb + ec)
    y = _dot(a_ref[...], w_ref[:A_WIDTH, :]) + _dot(b.astype(_BF16), w_ref[A_WIDTH:, :])
    out_ref[...] = x_ref[...] + y


def _even_out(x, a, outs, lses, w_out):
    t, d = x.shape
    tm = TOKEN_TILE
    tile = lambda w: pl.BlockSpec((tm, w), lambda i: (i, 0))
    return pl.pallas_call(
        _even_out_body,
        grid=(t // tm,),
        in_specs=[tile(d), tile(A_WIDTH)] + [tile(B_WIDTH)] * 6 + [_const_spec(w_out.shape)],
        out_specs=tile(d),
        out_shape=jax.ShapeDtypeStruct((t, d), _F32),
        compiler_params=_params("parallel"),
        name="even_out",
    )(x, a, *outs, *lses, w_out.astype(_BF16))


def _rope(x, cos, sin_signed, low):
    partner = jnp.where(low, pltpu.roll(x, LANES - C_ROPE // 2, 1), pltpu.roll(x, C_ROPE // 2, 1))
    return x * cos + partner * sin_signed


def _odd_in_body(x_ref, g_ref, win_ref, gcq_ref, gckv_ref, wqn_ref, wqr_ref, wkn_ref, wv_ref,
                 gqn_ref, gqr_ref, gkn_ref, gkr_ref, cos_ref, sin_ref, q_ref, k_ref, v_ref):
    x = x_ref[0]
    h = _rms(x, g_ref[...]).astype(_BF16)
    z = _dot(h, win_ref[...])
    cq = _rms(z[:, :C_Q_LORA], gcq_ref[...]).astype(_BF16)
    ckv = _rms(z[:, C_Q_LORA:C_Q_LORA + C_KV_LORA], gckv_ref[...]).astype(_BF16)
    kr = z[:, C_Q_LORA + C_KV_LORA:]
    qn = _dot(cq, wqn_ref[...])
    qr = _dot(cq, wqr_ref[...])
    kn = _dot(ckv, wkn_ref[...])
    vv = _dot(ckv, wv_ref[...])
    cos, sin = cos_ref[...], sin_ref[...]
    lane = lax.broadcasted_iota(jnp.int32, (1, LANES), 1)
    low = (lane % C_ROPE) < (C_ROPE // 2)
    inv = 1.0 / (C_NOPE + C_ROPE)
    scale = (C_NOPE + C_ROPE) ** -0.5 * math.log2(math.e)
    kr_ss = jnp.sum(kr * kr, axis=-1, keepdims=True)
    kr_roped = _rope(kr * gkr_ref[...], cos, sin, low)
    for hd in range(C_HEADS):
        sl = slice(hd * LANES, (hd + 1) * LANES)
        qnh, qrh, knh = qn[:, sl], qr[:, sl], kn[:, sl]
        ssq = jnp.sum(qnh * qnh, axis=-1, keepdims=True) + jnp.sum(qrh * qrh, axis=-1, keepdims=True)
        rq = lax.rsqrt(ssq * inv + EPS) * scale
        q_ref[0, hd, :, :LANES] = (qnh * rq * gqn_ref[...]).astype(q_ref.dtype)
        q_ref[0, hd, :, LANES:] = (_rope(qrh * gqr_ref[...], cos, sin, low) * rq).astype(q_ref.dtype)
        ssk = jnp.sum(knh * knh, axis=-1, keepdims=True) + kr_ss
        rk = lax.rsqrt(ssk * inv + EPS)
        k_ref[0, hd, :, :LANES] = (knh * rk * gkn_ref[...]).astype(k_ref.dtype)
        k_ref[0, hd, :, LANES:] = (kr_roped * rk).astype(k_ref.dtype)
        v_ref[0, hd, :, :C_V] = vv[:, sl].astype(v_ref.dtype)
        v_ref[0, hd, :, C_V:] = jnp.ones((x.shape[0], C_V), v_ref.dtype)


def _pad_lanes(a, width):
    return jnp.pad(a, [(0, 0)] * (a.ndim - 1) + [(0, width - a.shape[-1])])


def _odd_in(x, g, w_in, g_cq, g_ckv, w_uq, w_ukv, g_q, g_k, bsz):
    t, d = x.shape
    s = t // bsz
    tm = TOKEN_TILE
    w_in_p = _pad_lanes(w_in, C_Q_LORA + C_KV_LORA + LANES).astype(_BF16)
    wq = w_uq.reshape(C_Q_LORA, C_HEADS, C_NOPE + C_ROPE)
    wqn = wq[:, :, :C_NOPE].reshape(C_Q_LORA, C_HEADS * C_NOPE).astype(_BF16)
    wqr = _pad_lanes(wq[:, :, C_NOPE:], LANES).reshape(C_Q_LORA, C_HEADS * LANES).astype(_BF16)
    wkv = w_ukv.reshape(C_KV_LORA, C_HEADS, C_NOPE + C_V)
    wkn = wkv[:, :, :C_NOPE].reshape(C_KV_LORA, C_HEADS * C_NOPE).astype(_BF16)
    wv = wkv[:, :, C_NOPE:].reshape(C_KV_LORA, C_HEADS * C_V).astype(_BF16)
    half = C_ROPE // 2
    inv_freq = ROPE_THETA ** (-jnp.arange(half, dtype=_F32) / half)
    ang = jnp.arange(s, dtype=_F32)[:, None] * inv_freq[None, :]
    cos = jnp.tile(jnp.cos(ang), (1, LANES // half))
    sin = jnp.tile(jnp.concatenate([-jnp.sin(ang), jnp.sin(ang)], axis=-1), (1, LANES // C_ROPE))
    row = lambda a: a.reshape(1, -1)
    qk_shape = jax.ShapeDtypeStruct((bsz, C_HEADS, s, C_QK_PAD), _BF16)
    head_spec = lambda w: pl.BlockSpec((1, C_HEADS, tm, w), lambda b, j: (b, 0, j, 0))
    pos_spec = pl.BlockSpec((tm, LANES), lambda b, j: (j, 0))
    consts = [row(g), w_in_p, row(g_cq), row(g_ckv), wqn, wqr, wkn, wv,
              row(g_q[:C_NOPE]), row(_pad_lanes(g_q[C_NOPE:], LANES)),
              row(g_k[:C_NOPE]), row(_pad_lanes(g_k[C_NOPE:], LANES))]
    return pl.pallas_call(
        _odd_in_body,
        grid=(bsz, s // tm),
        in_specs=[pl.BlockSpec((1, tm, d), lambda b, j: (b, j, 0))]
        + [_const_spec(c.shape) for c in consts] + [pos_spec, pos_spec],
        out_specs=[head_spec(C_QK_PAD), head_spec(C_QK_PAD), head_spec(2 * C_V)],
        out_shape=[qk_shape, qk_shape, jax.ShapeDtypeStruct((bsz, C_HEADS, s, 2 * C_V), _BF16)],
        compiler_params=_params("parallel", "parallel"),
        name="odd_in",
    )(x.reshape(bsz, s, d), *consts, cos, sin)


FLASH_ROWS = 128
FLASH_TQ = 512


def _flash_body(q_ref, k_ref, v_ref, o_ref, s_ref, m_ref, acc_ref):
    qi = pl.program_id(2)
    tq = q_ref.shape[2]
    tk = tq
    dv = o_ref.shape[2]
    m_ref[...] = jnp.full_like(m_ref, NEG_INF)
    acc_ref[...] = jnp.zeros_like(acc_ref)

    def scores(j):
        start = pl.multiple_of(j * tk, tk)
        return _dot_nt(q_ref[0, 0], k_ref[0, 0, pl.ds(start, tk), :])

    def softmax_pv(s_all, j, diagonal):
        start = pl.multiple_of(j * tk, tk)
        v = v_ref[0, 0, pl.ds(start, tk), :]
        if diagonal:
            r = lax.broadcasted_iota(jnp.int32, s_all.shape, 0)
            col = lax.broadcasted_iota(jnp.int32, s_all.shape, 1)
            s_all = jnp.where(col <= r, s_all, NEG_INF)
        ps, scales = [], []
        for c in range(tq // FLASH_ROWS):
            rows = slice(c * FLASH_ROWS, (c + 1) * FLASH_ROWS)
            groups = [s_all[rows, g * LANES:(g + 1) * LANES] for g in range(tk // LANES)]
            m_old = m_ref[rows]
            gmax = functools.reduce(jnp.maximum, groups)
            m_new = jnp.maximum(m_old, jnp.max(gmax, axis=-1, keepdims=True))
            alpha = jnp.exp2(m_old - m_new)
            ps.append(jnp.concatenate([jnp.exp2((g - m_new).astype(_BF16)) for g in groups], axis=1))
            scales.append(jnp.concatenate([alpha] * (acc_ref.shape[1] // LANES), axis=1))
            m_ref[rows] = m_new
        pv = _dot(jnp.concatenate(ps, axis=0), v)
        acc_ref[...] = jnp.concatenate(scales, axis=0) * acc_ref[...] + pv

    s_ref[...] = scores(0)

    def full_tile(j, carry):
        s_cur = s_ref[...]
        s_next = scores(j + 1)
        softmax_pv(s_cur, j, False)
        s_ref[...] = s_next
        return carry

    lax.fori_loop(0, qi, full_tile, 0)
    softmax_pv(s_ref[...], qi, True)
    acc = acc_ref[...]
    o_ref[0] = (acc[:, :dv] / acc[:, dv:2 * dv]).astype(o_ref.dtype)


def _flash(q, k, v, tq):
    bsz, nh, s, dq = q.shape
    dv2 = v.shape[-1]
    dv = dv2 // 2
    whole = lambda w: pl.BlockSpec((1, 1, s, w), lambda b, h, i: (b, h, 0, 0))
    return pl.pallas_call(
        _flash_body,
        grid=(bsz, nh, s // tq),
        in_specs=[pl.BlockSpec((1, 1, tq, dq), lambda b, h, i: (b, h, i, 0)), whole(dq), whole(dv2)],
        out_specs=pl.BlockSpec((1, tq, dv), lambda b, h, i: (b, i, h)),
        out_shape=jax.ShapeDtypeStruct((bsz, s, nh * dv), _BF16),
        scratch_shapes=[pltpu.VMEM((tq, tq), _F32), pltpu.VMEM((tq, LANES), _F32), pltpu.VMEM((tq, dv2), _F32)],
        compiler_params=_params("parallel", "parallel", "arbitrary"),
        name="mla_flash",
    )(q, k, v)


def _proj_out_body(x_ref, o_ref, w_ref, out_ref):
    out_ref[...] = x_ref[...] + _dot(o_ref[...], w_ref[...])


def _proj_out(x, o, w_out):
    t, d = x.shape
    tm = TOKEN_TILE
    tile = lambda w: pl.BlockSpec((tm, w), lambda i: (i, 0))
    return pl.pallas_call(
        _proj_out_body,
        grid=(t // tm,),
        in_specs=[tile(d), tile(o.shape[1]), _const_spec(w_out.shape)],
        out_specs=tile(d),
        out_shape=jax.ShapeDtypeStruct((t, d), _F32),
        compiler_params=_params("parallel"),
        name="odd_out",
    )(x, o, w_out.astype(_BF16))


def _router_body(x_ref, g_ref, w_ref, b_ref, hp_ref, idx_ref, gate_ref, cnt_ref, base_ref):
    @pl.when(pl.program_id(0) == 0)
    def _():
        base_ref[...] = jnp.zeros_like(base_ref)

    h = _rms(x_ref[...], g_ref[...])
    tm, d = h.shape
    logits = jnp.dot(h, w_ref[...], preferred_element_type=_F32, precision=lax.Precision.HIGHEST) + b_ref[...]
    hb = h.astype(_BF16).astype(_F32)
    lo = lax.bitcast_convert_type(hb[:, :d // 2], jnp.uint32)
    hi = lax.bitcast_convert_type(hb[:, d // 2:], jnp.uint32)
    hp_ref[...] = (lo >> 16) | hi

    col = lax.broadcasted_iota(jnp.int32, logits.shape, 1)
    lane = lax.broadcasted_iota(jnp.int32, (tm, LANES), 1)
    r = lax.broadcasted_iota(jnp.int32, (tm, tm), 0)
    c = lax.broadcasted_iota(jnp.int32, (tm, tm), 1)
    before = (c < r).astype(_BF16)
    base = base_ref[...]
    idx_out = jnp.zeros((tm, LANES), jnp.int32)
    gate_out = jnp.zeros((tm, LANES), _F32)
    top0 = None
    denom = None
    work = logits
    for k in range(TOP_K):
        m = jnp.max(work, axis=-1, keepdims=True)
        sel = jnp.min(jnp.where(work == m, col, N_EXPERTS), axis=-1, keepdims=True)
        hit = col == sel
        work = jnp.where(hit, -jnp.inf, work)
        onehot = hit.astype(_F32)
        earlier = _dot(before, hit.astype(_BF16)) + base
        rank = jnp.sum(onehot * earlier, axis=-1, keepdims=True).astype(jnp.int32)
        base = base + jnp.sum(onehot, axis=0, keepdims=True)
        if k == 0:
            top0 = m
            e = jnp.ones_like(m)
            denom = e
        else:
            e = jnp.exp(m - top0)
            denom = denom + e
        idx_out = jnp.where(lane == k, sel, idx_out)
        idx_out = jnp.where(lane == TOP_K + k, rank, idx_out)
        gate_out = jnp.where(lane == k, e, gate_out)
    base_ref[...] = base
    cnt_ref[...] = base
    idx_ref[...] = idx_out
    gate_ref[...] = gate_out / denom


def _router(x, g, w_r, b_r):
    t, d = x.shape
    tm = TOKEN_TILE
    tile = lambda w: pl.BlockSpec((tm, w), lambda i: (i, 0))
    return pl.pallas_call(
        _router_body,
        grid=(t // tm,),
        in_specs=[tile(d), _const_spec((1, d)), _const_spec(w_r.shape), _const_spec((1, N_EXPERTS))],
        out_specs=[tile(d // 2), tile(LANES), tile(LANES), _const_spec((1, N_EXPERTS))],
        out_shape=[jax.ShapeDtypeStruct((t, d // 2), jnp.uint32), jax.ShapeDtypeStruct((t, LANES), jnp.int32),
                   jax.ShapeDtypeStruct((t, LANES), _F32), jax.ShapeDtypeStruct((1, N_EXPERTS), _F32)],
        scratch_shapes=[pltpu.VMEM((1, N_EXPERTS), _F32)],
        compiler_params=_params("arbitrary"),
        name="moe_router",
    )(x, g.reshape(1, d), w_r, b_r.reshape(1, N_EXPERTS))


def _dispatch_body(slot_ref, hp_ref, xb_in_ref, xb_ref, sem):
    del xb_in_ref
    tm = hp_ref.shape[0]

    def row_copy(r, s):
        return pltpu.make_async_copy(hp_ref.at[pl.ds(r, 1)], xb_ref.at[pl.ds(s, 1)], sem)

    def start(r, carry):
        for k in range(TOP_K):
            row_copy(r, slot_ref[r * TOP_K + k]).start()
        return carry

    def wait(r, carry):
        for k in range(TOP_K):
            row_copy(r, slot_ref[r * TOP_K + k]).wait()
        return carry

    lax.fori_loop(0, tm, start, 0)
    lax.fori_loop(0, tm, wait, 0)


def _dispatch(hp, slot_flat, n_rows):
    t, w = hp.shape
    tm = TOKEN_TILE
    return pl.pallas_call(
        _dispatch_body,
        grid=(t // tm,),
        in_specs=[pl.BlockSpec((tm * TOP_K,), lambda i: (i,), memory_space=pltpu.SMEM),
                  pl.BlockSpec((tm, w), lambda i: (i, 0)),
                  pl.BlockSpec(memory_space=pl.ANY)],
        out_specs=pl.BlockSpec(memory_space=pl.ANY),
        out_shape=jax.ShapeDtypeStruct((n_rows, w), hp.dtype),
        scratch_shapes=[pltpu.SemaphoreType.DMA(())],
        input_output_aliases={2: 0},
        compiler_params=_params("arbitrary"),
        name="moe_dispatch",
    )(slot_flat, hp, jnp.zeros((n_rows, w), hp.dtype))


def _expert_body(be_ref, nu_ref, xb_ref, w1_ref, b1g_ref, b1l_ref, w2_ref, b2_ref, perm_ref,
                 yb_ref, w1g_s, w1l_s, w2_s):
    b = pl.program_id(0)
    live = b < nu_ref[0]
    changed = (b == 0) | (be_ref[b] != be_ref[jnp.maximum(b - 1, 0)])

    @pl.when(live & changed)
    def _():
        group = perm_ref.shape[0]
        for c in range(w1_ref.shape[2] // group):
            wc = w1_ref[0, :, c * group:(c + 1) * group].astype(_BF16)
            d = _dot(wc, perm_ref[...]).astype(_BF16)
            w1g_s[:, c * (group // 2):(c + 1) * (group // 2)] = d[:, :group // 2]
            w1l_s[:, c * (group // 2):(c + 1) * (group // 2)] = d[:, group // 2:]
        w2_s[...] = w2_ref[0].astype(_BF16)

    @pl.when(live)
    def _():
        words = xb_ref[...]
        lo = lax.bitcast_convert_type(words << 16, _F32).astype(_BF16)
        hi = lax.bitcast_convert_type(words & jnp.uint32(0xFFFF0000), _F32).astype(_BF16)
        x = jnp.concatenate([lo, hi], axis=1)
        glu = _dot(x, w1g_s[...]) + b1g_ref[0]
        lin = _dot(x, w1l_s[...]) + b1l_ref[0]
        glu = jnp.minimum(glu, SWIGLU_LIMIT)
        lin = jnp.clip(lin, -SWIGLU_LIMIT, SWIGLU_LIMIT)
        act = glu * jax.nn.sigmoid(SWIGLU_ALPHA * glu) * (lin + 1.0)
        yb_ref[...] = _dot(act.astype(_BF16), w2_s[...]) + b2_ref[0]

    @pl.when(jnp.logical_not(live))
    def _():
        yb_ref[...] = jnp.zeros_like(yb_ref)


def _experts(xb, block_expert, n_used, layer, w1, b1, w2, b2):
    n_rows, half_d = xb.shape
    d = 2 * half_d
    n_blocks = n_rows // MOE_BLOCK
    _, n_e, _, two_f = w1.shape
    f = two_f // 2
    group = 2 * LANES
    j = jnp.arange(group)
    src = jnp.where(j < LANES, 2 * j, 2 * (j - LANES) + 1)
    perm = (jnp.arange(group)[:, None] == src[None, :]).astype(_BF16)
    b1p = b1.reshape(n_e, 1, f, 2)
    blk = lambda b, be, nu: jnp.minimum(b, nu[0] - 1)
    e_spec = lambda shape: pl.BlockSpec((1,) + shape, lambda b, be, nu: (be[b],) + (0,) * len(shape))
    w_spec = lambda shape: pl.BlockSpec((None, 1) + shape, lambda b, be, nu: (layer, be[b], 0, 0))
    grid_spec = pltpu.PrefetchScalarGridSpec(
        num_scalar_prefetch=2,
        grid=(n_blocks,),
        in_specs=[pl.BlockSpec((MOE_BLOCK, half_d), lambda b, be, nu: (blk(b, be, nu), 0)),
                  w_spec((d, two_f)), e_spec((1, f)), e_spec((1, f)), w_spec((f, d)), e_spec((1, d)),
                  pl.BlockSpec((group, group), lambda b, be, nu: (0, 0))],
        out_specs=pl.BlockSpec((MOE_BLOCK, d), lambda b, be, nu: (b, 0)),
        scratch_shapes=[pltpu.VMEM((d, f), _BF16), pltpu.VMEM((d, f), _BF16), pltpu.VMEM((f, d), _BF16)],
    )
    return pl.pallas_call(
        _expert_body,
        grid_spec=grid_spec,
        out_shape=jax.ShapeDtypeStruct((n_rows, d), _F32),
        compiler_params=_params("arbitrary"),
        name="moe_experts",
    )(block_expert, n_used, xb, w1, b1p[..., 0], b1p[..., 1], w2, b2.reshape(n_e, 1, d), perm)


def _combine_body(slot_ref, x_ref, gate_ref, yb_ref, out_ref, rows_ref, sem):
    tm = x_ref.shape[0]

    def row_copy(r, k):
        s = slot_ref[r * TOP_K + k]
        return pltpu.make_async_copy(yb_ref.at[pl.ds(s, 1)], rows_ref.at[k, pl.ds(r, 1)], sem)

    def start(r, carry):
        for k in range(TOP_K):
            row_copy(r, k).start()
        return carry

    def wait(r, carry):
        for k in range(TOP_K):
            row_copy(r, k).wait()
        return carry

    lax.fori_loop(0, tm, start, 0)
    lax.fori_loop(0, tm, wait, 0)
    acc = x_ref[...]
    for k in range(TOP_K):
        acc = acc + gate_ref[:, k:k + 1] * rows_ref[k]
    out_ref[...] = acc


def _combine(x, gates, slot_flat, yb):
    t, d = x.shape
    tm = TOKEN_TILE
    return pl.pallas_call(
        _combine_body,
        grid=(t // tm,),
        in_specs=[pl.BlockSpec((tm * TOP_K,), lambda i: (i,), memory_space=pltpu.SMEM),
                  pl.BlockSpec((tm, d), lambda i: (i, 0)),
                  pl.BlockSpec((tm, LANES), lambda i: (i, 0)),
                  pl.BlockSpec(memory_space=pl.ANY)],
        out_specs=pl.BlockSpec((tm, d), lambda i: (i, 0)),
        out_shape=jax.ShapeDtypeStruct((t, d), _F32),
        scratch_shapes=[pltpu.VMEM((TOP_K, tm, d), _F32), pltpu.SemaphoreType.DMA(())],
        compiler_params=_params("arbitrary"),
        name="moe_combine",
    )(slot_flat, x, gates, yb)


def _moe(x, g, w_r, b_r, layer, w1, b1, w2, b2):
    t = x.shape[0]
    n_blocks = -(-(t * TOP_K + N_EXPERTS * (MOE_BLOCK - 1)) // MOE_BLOCK)
    hp, idx, gates, counts = _router(x, g, w_r, b_r)
    counts = counts[0].astype(jnp.int32)
    padded = (counts + MOE_BLOCK - 1) // MOE_BLOCK * MOE_BLOCK
    pad_end = jnp.cumsum(padded)
    pad_start = pad_end - padded
    experts = jnp.arange(N_EXPERTS, dtype=jnp.int32)
    start_of = jnp.sum(jnp.where(idx[:, :TOP_K, None] == experts, pad_start, 0), axis=-1)
    slot_flat = (start_of + idx[:, TOP_K:2 * TOP_K]).reshape(-1).astype(jnp.int32)
    n_used = (pad_end[-1] // MOE_BLOCK).astype(jnp.int32)
    block_id = jnp.minimum(jnp.arange(n_blocks, dtype=jnp.int32), n_used - 1)
    block_expert = jnp.sum(pad_end[None, :] <= (block_id * MOE_BLOCK)[:, None], axis=-1).astype(jnp.int32)
    xb = _dispatch(hp, slot_flat, n_blocks * MOE_BLOCK)
    yb = _experts(xb, block_expert, n_used.reshape(1), layer, w1, b1, w2, b2)
    return _combine(x, gates, slot_flat, yb)


def kernel(x, mix_norm, ffn_norm, even_w_in, even_w_s, even_b_s, even_g_v, even_g_q, even_g_k, even_w_out, odd_w_in, odd_g_cq, odd_g_ckv, odd_w_uq, odd_w_ukv, odd_g_q, odd_g_k, odd_w_out, router_w, router_b, expert_w1, expert_b1, expert_w2, expert_b2):
    bsz, s, d = x.shape
    xt = x.reshape(bsz * s, d)
    for l in range(mix_norm.shape[0]):
        i = l // 2
        if l % 2 == 0:
            a, q, k, v = _even_in(xt, mix_norm[l], even_w_in[i], even_w_s[i], even_b_s[i], even_g_v[i],
                                  even_g_q[i], even_g_k[i])
            branches = [_dilated_branch(q, k, v, bsz, dil) for _, dil in B_BRANCHES]
            xt = _even_out(xt, a, [o for o, _ in branches], [lse for _, lse in branches], even_w_out[i])
        else:
            q, k, v = _odd_in(xt, mix_norm[l], odd_w_in[i], odd_g_cq[i], odd_g_ckv[i], odd_w_uq[i],
                              odd_w_ukv[i], odd_g_q[i], odd_g_k[i], bsz)
            o = _flash(q, k, v, FLASH_TQ)
            xt = _proj_out(xt, o.reshape(bsz * s, -1), odd_w_out[i])
        xt = _moe(xt, ffn_norm[l], router_w[l], router_b[l], l, expert_w1, expert_b1[l],
                  expert_w2, expert_b2[l])
    return xt.reshape(bsz, s, d)
```

```python
import functools
import math

import jax
import jax.numpy as jnp
from jax import lax
from jax.experimental import pallas as pl
from jax.experimental.pallas import tpu as pltpu

EPS = 1e-6
NEG_INF = -1e30
A_HEADS = 4
A_HEAD_DIM = 128
A_CHUNK = 128
A_WIDTH = A_HEADS * A_HEAD_DIM
B_HEADS = 8
B_HEAD_DIM = 64
B_WIDTH = B_HEADS * B_HEAD_DIM
B_BRANCHES = ((128, 1), (512, 4), (2048, 16))
B_BLOCK = 128
C_HEADS = 8
C_NOPE = 128
C_ROPE = 64
C_V = 128
C_Q_LORA = 512
C_KV_LORA = 256
C_QK_PAD = 256
ROPE_THETA = 10000.0
N_EXPERTS = 32
TOP_K = 4
D_EXPERT = 1024
SWIGLU_ALPHA = 1.702
SWIGLU_LIMIT = 7.0
MOE_BLOCK = 512

LANES = 128
TOKEN_TILE = 256
VMEM_LIMIT = 56 * 1024 * 1024

_F32 = jnp.float32
_BF16 = jnp.bfloat16


def _params(*sem):
    return pltpu.CompilerParams(dimension_semantics=sem, vmem_limit_bytes=VMEM_LIMIT)


def _rms(x, g):
    return x * lax.rsqrt(jnp.mean(x * x, axis=-1, keepdims=True) + EPS) * g


def _gelu(x):
    return 0.5 * x * (1.0 + lax.erf(x * math.sqrt(0.5)))


def _dot(a, b):
    return jnp.dot(a, b, preferred_element_type=_F32)


def _dot_nt(a, b):
    return lax.dot_general(a, b, (((1,), (1,)), ((), ())), preferred_element_type=_F32)


def _seg_sum(x2, seg):
    hi = x2.astype(_BF16)
    lo = (x2 - hi.astype(_F32)).astype(_BF16)
    return _dot(hi, seg) + _dot(lo, seg)


def _const_spec(shape):
    nd = len(shape)
    return pl.BlockSpec(shape, lambda *_: (0,) * nd)


def _even_in_body(x_ref, g_ref, win_ref, ws_ref, bst_ref, gv_ref, gq_ref, gk_ref, seg_ref,
                  a_ref, *rest):
    qkv_refs, stage_ref = rest[:-1], rest[-1]
    x = x_ref[0]
    h = _rms(x, g_ref[...]).astype(_BF16)
    z = _dot(h, win_ref[...])
    tm = x.shape[0]
    u = _gelu(z[:, :A_WIDTH])
    vv = _gelu(z[:, A_WIDTH:2 * A_WIDTH])
    row = lax.broadcasted_iota(jnp.int32, (A_CHUNK, A_CHUNK), 0)
    col = lax.broadcasted_iota(jnp.int32, (A_CHUNK, A_CHUNK), 1)
    for hd in range(A_HEADS):
        sl = slice(hd * A_HEAD_DIM, (hd + 1) * A_HEAD_DIM)
        vh = _rms(vv[:, sl], gv_ref[hd:hd + 1, :]).astype(_BF16)
        w = jnp.where(row >= col, ws_ref[hd], 0.0).astype(_BF16)
        bias = bst_ref[:, hd:hd + 1]
        for c in range(tm // A_CHUNK):
            rs = slice(c * A_CHUNK, (c + 1) * A_CHUNK)
            mixed = _dot(w, vh[rs]) + bias
            a_ref[0, rs, sl] = (u[rs, sl] * mixed).astype(a_ref.dtype)
    o = 2 * A_WIDTH
    q = z[:, o:o + B_WIDTH]
    k = z[:, o + B_WIDTH:o + 2 * B_WIDTH]
    seg = seg_ref[...]
    inv = 1.0 / B_HEAD_DIM
    qn = q * lax.rsqrt(_seg_sum(q * q, seg) * inv + EPS) * gq_ref[...]
    kn = k * lax.rsqrt(_seg_sum(k * k, seg) * inv + EPS) * gk_ref[...]
    qkv = (qn * (B_HEAD_DIM ** -0.5), kn, z[:, o + 2 * B_WIDTH:])
    groups = B_WIDTH // LANES
    for which, val in enumerate(qkv):
        for gi in range(groups):
            stage_ref[gi] = val[:, gi * LANES:(gi + 1) * LANES]
        for bi, (_, dil) in enumerate(B_BRANCHES):
            ref = qkv_refs[3 * bi + which]
            if dil == 1:
                ref[0] = val.astype(ref.dtype)
                continue
            for r in range(dil):
                for gi in range(groups):
                    rows = stage_ref[gi, pl.ds(r, tm // dil, stride=dil), :]
                    lo = r * B_WIDTH + gi * LANES
                    ref[0, :, lo:lo + LANES] = rows.astype(ref.dtype)


def _class_view_spec(tm, dil):
    return pl.BlockSpec((1, tm // dil, dil * B_WIDTH), lambda b, j: (b, j, 0))


def _even_in(x, g, w_in, w_s, b_s, g_v, g_q, g_k, bsz):
    t, d = x.shape
    s = t // bsz
    tm = TOKEN_TILE
    seg = (jnp.arange(B_WIDTH)[:, None] // B_HEAD_DIM == jnp.arange(B_WIDTH)[None, :] // B_HEAD_DIM).astype(_BF16)
    tile = lambda w: pl.BlockSpec((1, tm, w), lambda b, j: (b, j, 0))
    qkv_specs, qkv_shapes = [], []
    for _, dil in B_BRANCHES:
        qkv_specs += [_class_view_spec(tm, dil)] * 3
        qkv_shapes += [jax.ShapeDtypeStruct((bsz, s // dil, dil * B_WIDTH), _BF16)] * 3
    outs = pl.pallas_call(
        _even_in_body,
        grid=(bsz, s // tm),
        in_specs=[tile(d), _const_spec((1, d)), _const_spec(w_in.shape), _const_spec(w_s.shape),
                  _const_spec((A_CHUNK, A_HEADS)), _const_spec(g_v.shape), _const_spec((1, B_WIDTH)),
                  _const_spec((1, B_WIDTH)), _const_spec(seg.shape)],
        out_specs=[tile(A_WIDTH)] + qkv_specs,
        out_shape=[jax.ShapeDtypeStruct((bsz, s, A_WIDTH), _BF16)] + qkv_shapes,
        scratch_shapes=[pltpu.VMEM((B_WIDTH // LANES, tm, LANES), _F32)],
        compiler_params=_params("parallel", "parallel"),
        name="even_in",
    )(x.reshape(bsz, s, d), g.reshape(1, d), w_in.astype(_BF16), w_s, b_s.T, g_v,
      jnp.tile(g_q, B_HEADS).reshape(1, B_WIDTH), jnp.tile(g_k, B_HEADS).reshape(1, B_WIDTH), seg)
    return outs[0], [outs[1 + 3 * bi:4 + 3 * bi] for bi in range(len(B_BRANCHES))]


def _dilated_body(q_ref, kp_ref, kc_ref, vp_ref, vc_ref, o_ref, lse_ref):
    jb = pl.program_id(2)
    q = q_ref[0]
    kk = jnp.concatenate([kp_ref[0], kc_ref[0]], axis=0)
    vv = jnp.concatenate([vp_ref[0], vc_ref[0]], axis=0)
    i = lax.broadcasted_iota(jnp.int32, (B_BLOCK, 2 * B_BLOCK), 0)
    c = lax.broadcasted_iota(jnp.int32, (B_BLOCK, 2 * B_BLOCK), 1)
    valid = (c >= i) & (c <= i + B_BLOCK) & ((jb > 0) | (c >= B_BLOCK))
    lane = lax.broadcasted_iota(jnp.int32, (1, LANES), 1)
    first = lane < B_HEAD_DIM
    for p in range(B_WIDTH // LANES):
        sl = slice(p * LANES, (p + 1) * LANES)
        q2, k2, v2 = q[:, sl], kk[:, sl], vv[:, sl]
        outs, lses = [], []
        for half in (first, jnp.logical_not(first)):
            s = _dot_nt(jnp.where(half, q2, jnp.zeros_like(q2)), k2)
            s = jnp.where(valid, s, NEG_INF)
            m = jnp.max(s, axis=-1, keepdims=True)
            e = jnp.exp(s - m)
            l = jnp.sum(e, axis=-1, keepdims=True)
            outs.append(_dot(e.astype(_BF16), v2) / l)
            lses.append(m + jnp.log(l))
        o_ref[0, :, sl] = jnp.where(first, outs[0], outs[1])
        lse_ref[0, :, sl] = jnp.where(first, lses[0], lses[1])


def _dilated_branch(q, k, v, dil):
    bsz, n, _ = q.shape
    cur = pl.BlockSpec((1, B_BLOCK, B_WIDTH), lambda b, r, j: (b, j, r))
    prev = pl.BlockSpec((1, B_BLOCK, B_WIDTH), lambda b, r, j: (b, jnp.maximum(j - 1, 0), r))
    shape = jax.ShapeDtypeStruct((bsz, n, dil * B_WIDTH), _F32)
    return pl.pallas_call(
        _dilated_body,
        grid=(bsz, dil, n // B_BLOCK),
        in_specs=[cur, prev, cur, prev, cur],
        out_specs=[cur, cur],
        out_shape=[shape, shape],
        compiler_params=_params("parallel", "parallel", "parallel"),
        name=f"dilated_d{dil}",
    )(q, k, k, v, v)


def _even_out_body(x_ref, a_ref, *rest):
    nb = len(B_BRANCHES)
    branch_refs, w_ref, out_ref, stage_ref = rest[:2 * nb], rest[2 * nb], rest[2 * nb + 1], rest[2 * nb + 2]
    tm = x_ref.shape[1]

    def token_order(ref, dil, slot):
        if dil == 1:
            return ref[0]
        groups = B_WIDTH // LANES
        for r in range(dil):
            for gi in range(groups):
                lo = r * B_WIDTH + gi * LANES
                stage_ref[slot, gi, pl.ds(r, tm // dil, stride=dil), :] = ref[0, :, lo:lo + LANES]
        return jnp.concatenate([stage_ref[slot, gi] for gi in range(groups)], axis=1)

    outs = [token_order(branch_refs[2 * bi], dil, 2 * bi) for bi, (_, dil) in enumerate(B_BRANCHES)]
    lses = [token_order(branch_refs[2 * bi + 1], dil, 2 * bi + 1) for bi, (_, dil) in enumerate(B_BRANCHES)]
    m = functools.reduce(jnp.maximum, lses)
    es = [jnp.exp(l - m) for l in lses]
    b = sum(e * o for e, o in zip(es, outs)) / sum(es)
    y = _dot(a_ref[0], w_ref[:A_WIDTH, :]) + _dot(b.astype(_BF16), w_ref[A_WIDTH:, :])
    out_ref[0] = x_ref[0] + y


def _even_out(x, a, branches, w_out):
    t, d = x.shape
    bsz, s, _ = a.shape
    tm = TOKEN_TILE
    tile = lambda w: pl.BlockSpec((1, tm, w), lambda b, j: (b, j, 0))
    branch_specs = []
    for _, dil in B_BRANCHES:
        branch_specs += [_class_view_spec(tm, dil)] * 2
    out = pl.pallas_call(
        _even_out_body,
        grid=(bsz, s // tm),
        in_specs=[tile(d), tile(A_WIDTH)] + branch_specs + [_const_spec(w_out.shape)],
        out_specs=tile(d),
        out_shape=jax.ShapeDtypeStruct((bsz, s, d), _F32),
        scratch_shapes=[pltpu.VMEM((2 * len(B_BRANCHES), B_WIDTH // LANES, tm, LANES), _F32)],
        compiler_params=_params("parallel", "parallel"),
        name="even_out",
    )(x.reshape(bsz, s, d), a, *[arr for pair in branches for arr in pair], w_out.astype(_BF16))
    return out.reshape(t, d)


def _rope(x, cos, sin_signed, low):
    partner = jnp.where(low, pltpu.roll(x, LANES - C_ROPE // 2, 1), pltpu.roll(x, C_ROPE // 2, 1))
    return x * cos + partner * sin_signed


def _odd_in_body(x_ref, g_ref, win_ref, gcq_ref, gckv_ref, wqn_ref, wqr_ref, wkn_ref, wv_ref,
                 gqn_ref, gqr_ref, gkn_ref, gkr_ref, cos_ref, sin_ref, q_ref, k_ref, v_ref):
    x = x_ref[0]
    h = _rms(x, g_ref[...]).astype(_BF16)
    z = _dot(h, win_ref[...])
    cq = _rms(z[:, :C_Q_LORA], gcq_ref[...]).astype(_BF16)
    ckv = _rms(z[:, C_Q_LORA:C_Q_LORA + C_KV_LORA], gckv_ref[...]).astype(_BF16)
    kr = z[:, C_Q_LORA + C_KV_LORA:]
    qn = _dot(cq, wqn_ref[...])
    qr = _dot(cq, wqr_ref[...])
    kn = _dot(ckv, wkn_ref[...])
    vv = _dot(ckv, wv_ref[...])
    cos, sin = cos_ref[...], sin_ref[...]
    lane = lax.broadcasted_iota(jnp.int32, (1, LANES), 1)
    low = (lane % C_ROPE) < (C_ROPE // 2)
    inv = 1.0 / (C_NOPE + C_ROPE)
    scale = (C_NOPE + C_ROPE) ** -0.5 * math.log2(math.e)
    kr_ss = jnp.sum(kr * kr, axis=-1, keepdims=True)
    kr_roped = _rope(kr * gkr_ref[...], cos, sin, low)
    for hd in range(C_HEADS):
        sl = slice(hd * LANES, (hd + 1) * LANES)
        qnh, qrh, knh = qn[:, sl], qr[:, sl], kn[:, sl]
        ssq = jnp.sum(qnh * qnh, axis=-1, keepdims=True) + jnp.sum(qrh * qrh, axis=-1, keepdims=True)
        rq = lax.rsqrt(ssq * inv + EPS) * scale
        q_ref[0, hd, :, :LANES] = (qnh * rq * gqn_ref[...]).astype(q_ref.dtype)
        q_ref[0, hd, :, LANES:] = (_rope(qrh * gqr_ref[...], cos, sin, low) * rq).astype(q_ref.dtype)
        ssk = jnp.sum(knh * knh, axis=-1, keepdims=True) + kr_ss
        rk = lax.rsqrt(ssk * inv + EPS)
        k_ref[0, hd, :, :LANES] = (knh * rk * gkn_ref[...]).astype(k_ref.dtype)
        k_ref[0, hd, :, LANES:] = (kr_roped * rk).astype(k_ref.dtype)
        v_ref[0, hd, :, :C_V] = vv[:, sl].astype(v_ref.dtype)
        v_ref[0, hd, :, C_V:] = jnp.ones((x.shape[0], C_V), v_ref.dtype)


def _pad_lanes(a, width):
    return jnp.pad(a, [(0, 0)] * (a.ndim - 1) + [(0, width - a.shape[-1])])


def _odd_in(x, g, w_in, g_cq, g_ckv, w_uq, w_ukv, g_q, g_k, bsz):
    t, d = x.shape
    s = t // bsz
    tm = TOKEN_TILE
    w_in_p = _pad_lanes(w_in, C_Q_LORA + C_KV_LORA + LANES).astype(_BF16)
    wq = w_uq.reshape(C_Q_LORA, C_HEADS, C_NOPE + C_ROPE)
    wqn = wq[:, :, :C_NOPE].reshape(C_Q_LORA, C_HEADS * C_NOPE).astype(_BF16)
    wqr = _pad_lanes(wq[:, :, C_NOPE:], LANES).reshape(C_Q_LORA, C_HEADS * LANES).astype(_BF16)
    wkv = w_ukv.reshape(C_KV_LORA, C_HEADS, C_NOPE + C_V)
    wkn = wkv[:, :, :C_NOPE].reshape(C_KV_LORA, C_HEADS * C_NOPE).astype(_BF16)
    wv = wkv[:, :, C_NOPE:].reshape(C_KV_LORA, C_HEADS * C_V).astype(_BF16)
    half = C_ROPE // 2
    inv_freq = ROPE_THETA ** (-jnp.arange(half, dtype=_F32) / half)
    ang = jnp.arange(s, dtype=_F32)[:, None] * inv_freq[None, :]
    cos = jnp.tile(jnp.cos(ang), (1, LANES // half))
    sin = jnp.tile(jnp.concatenate([-jnp.sin(ang), jnp.sin(ang)], axis=-1), (1, LANES // C_ROPE))
    row = lambda a: a.reshape(1, -1)
    qk_shape = jax.ShapeDtypeStruct((bsz, C_HEADS, s, C_QK_PAD), _BF16)
    head_spec = lambda w: pl.BlockSpec((1, C_HEADS, tm, w), lambda b, j: (b, 0, j, 0))
    pos_spec = pl.BlockSpec((tm, LANES), lambda b, j: (j, 0))
    consts = [row(g), w_in_p, row(g_cq), row(g_ckv), wqn, wqr, wkn, wv,
              row(g_q[:C_NOPE]), row(_pad_lanes(g_q[C_NOPE:], LANES)),
              row(g_k[:C_NOPE]), row(_pad_lanes(g_k[C_NOPE:], LANES))]
    return pl.pallas_call(
        _odd_in_body,
        grid=(bsz, s // tm),
        in_specs=[pl.BlockSpec((1, tm, d), lambda b, j: (b, j, 0))]
        + [_const_spec(c.shape) for c in consts] + [pos_spec, pos_spec],
        out_specs=[head_spec(C_QK_PAD), head_spec(C_QK_PAD), head_spec(2 * C_V)],
        out_shape=[qk_shape, qk_shape, jax.ShapeDtypeStruct((bsz, C_HEADS, s, 2 * C_V), _BF16)],
        compiler_params=_params("parallel", "parallel"),
        name="odd_in",
    )(x.reshape(bsz, s, d), *consts, cos, sin)


FLASH_ROWS = 128
FLASH_TQ = 1024


def _flash_body(q_ref, k_ref, v_ref, o_ref, s_ref, m_ref, acc_ref):
    qi = pl.program_id(2)
    tq = q_ref.shape[2]
    tk = tq
    dv = o_ref.shape[2]
    m_ref[...] = jnp.full_like(m_ref, NEG_INF)
    acc_ref[...] = jnp.zeros_like(acc_ref)

    def scores(j):
        start = pl.multiple_of(j * tk, tk)
        return _dot_nt(q_ref[0, 0], k_ref[0, 0, pl.ds(start, tk), :])

    def softmax_pv(s_all, j, diagonal):
        start = pl.multiple_of(j * tk, tk)
        v = v_ref[0, 0, pl.ds(start, tk), :]
        if diagonal:
            r = lax.broadcasted_iota(jnp.int32, s_all.shape, 0)
            col = lax.broadcasted_iota(jnp.int32, s_all.shape, 1)
            s_all = jnp.where(col <= r, s_all, NEG_INF)
        ps, scales = [], []
        for c in range(tq // FLASH_ROWS):
            rows = slice(c * FLASH_ROWS, (c + 1) * FLASH_ROWS)
            groups = [s_all[rows, g * LANES:(g + 1) * LANES] for g in range(tk // LANES)]
            m_old = m_ref[rows]
            gmax = functools.reduce(jnp.maximum, groups)
            m_new = jnp.maximum(m_old, jnp.max(gmax, axis=-1, keepdims=True))
            alpha = jnp.exp2(m_old - m_new)
            ps.append(jnp.concatenate([jnp.exp2((g - m_new).astype(_BF16)) for g in groups], axis=1))
            scales.append(jnp.concatenate([alpha] * (acc_ref.shape[1] // LANES), axis=1))
            m_ref[rows] = m_new
        pv = _dot(jnp.concatenate(ps, axis=0), v)
        acc_ref[...] = jnp.concatenate(scales, axis=0) * acc_ref[...] + pv

    s_ref[...] = scores(0)

    def full_tile(j, carry):
        s_cur = s_ref[...]
        s_next = scores(j + 1)
        softmax_pv(s_cur, j, False)
        s_ref[...] = s_next
        return carry

    lax.fori_loop(0, qi, full_tile, 0)
    softmax_pv(s_ref[...], qi, True)
    acc = acc_ref[...]
    o_ref[0] = (acc[:, :dv] / acc[:, dv:2 * dv]).astype(o_ref.dtype)


def _flash(q, k, v, tq):
    bsz, nh, s, dq = q.shape
    dv2 = v.shape[-1]
    dv = dv2 // 2
    whole = lambda w: pl.BlockSpec((1, 1, s, w), lambda b, h, i: (b, h, 0, 0))
    return pl.pallas_call(
        _flash_body,
        grid=(bsz, nh, s // tq),
        in_specs=[pl.BlockSpec((1, 1, tq, dq), lambda b, h, i: (b, h, i, 0)), whole(dq), whole(dv2)],
        out_specs=pl.BlockSpec((1, tq, dv), lambda b, h, i: (b, i, h)),
        out_shape=jax.ShapeDtypeStruct((bsz, s, nh * dv), _BF16),
        scratch_shapes=[pltpu.VMEM((tq, tq), _F32), pltpu.VMEM((tq, LANES), _F32), pltpu.VMEM((tq, dv2), _F32)],
        compiler_params=_params("parallel", "parallel", "arbitrary"),
        name="mla_flash",
    )(q, k, v)


def _proj_out_body(x_ref, o_ref, w_ref, out_ref):
    out_ref[...] = x_ref[...] + _dot(o_ref[...], w_ref[...])


def _proj_out(x, o, w_out):
    t, d = x.shape
    tm = TOKEN_TILE
    tile = lambda w: pl.BlockSpec((tm, w), lambda i: (i, 0))
    return pl.pallas_call(
        _proj_out_body,
        grid=(t // tm,),
        in_specs=[tile(d), tile(o.shape[1]), _const_spec(w_out.shape)],
        out_specs=tile(d),
        out_shape=jax.ShapeDtypeStruct((t, d), _F32),
        compiler_params=_params("parallel"),
        name="odd_out",
    )(x, o, w_out.astype(_BF16))


SUBLANES = 8


def _store_row_tiles(ref, val, lead=()):
    n = val.shape[0]
    for j in range(SUBLANES):
        ref[lead + (pl.ds(j, n, stride=SUBLANES), slice(None))] = val[:, j * LANES:(j + 1) * LANES]


def _load_row_tile_column(ref, n, j, lead=()):
    return ref[lead + (pl.ds(j, n, stride=SUBLANES), slice(None))]


def _row_tile(ref, row):
    return ref.at[pl.ds(pl.multiple_of(row * SUBLANES, SUBLANES), SUBLANES)]


def _router_body(x_ref, g_ref, w_ref, b_ref, hp_ref, idx_ref, gate_ref, cnt_ref, base_ref):
    @pl.when(pl.program_id(0) == 0)
    def _():
        base_ref[...] = jnp.zeros_like(base_ref)

    h = _rms(x_ref[...], g_ref[...])
    tm, d = h.shape
    w = w_ref[...]
    h_hi, w_hi = h.astype(_BF16), w.astype(_BF16)
    h_lo = (h - h_hi.astype(_F32)).astype(_BF16)
    w_lo = (w - w_hi.astype(_F32)).astype(_BF16)
    logits = _dot(h_hi, w_hi) + _dot(h_lo, w_hi) + _dot(h_hi, w_lo) + b_ref[...]
    _store_row_tiles(hp_ref, h)

    col = lax.broadcasted_iota(jnp.int32, logits.shape, 1)
    lane = lax.broadcasted_iota(jnp.int32, (tm, LANES), 1)
    r = lax.broadcasted_iota(jnp.int32, (tm, tm), 0)
    c = lax.broadcasted_iota(jnp.int32, (tm, tm), 1)
    before = (c < r).astype(_BF16)
    base = base_ref[...]
    idx_out = jnp.zeros((tm, LANES), jnp.int32)
    gate_out = jnp.zeros((tm, LANES), _F32)
    top0 = None
    denom = None
    work = logits
    for k in range(TOP_K):
        m = jnp.max(work, axis=-1, keepdims=True)
        sel = jnp.min(jnp.where(work == m, col, N_EXPERTS), axis=-1, keepdims=True)
        hit = col == sel
        work = jnp.where(hit, -jnp.inf, work)
        onehot = hit.astype(_F32)
        earlier = _dot(before, hit.astype(_BF16)) + base
        rank = jnp.sum(onehot * earlier, axis=-1, keepdims=True).astype(jnp.int32)
        base = base + jnp.sum(onehot, axis=0, keepdims=True)
        if k == 0:
            top0 = m
            e = jnp.ones_like(m)
            denom = e
        else:
            e = jnp.exp(m - top0)
            denom = denom + e
        idx_out = jnp.where(lane == k, sel, idx_out)
        idx_out = jnp.where(lane == TOP_K + k, rank, idx_out)
        gate_out = jnp.where(lane == k, e, gate_out)
    base_ref[...] = base
    cnt_ref[...] = base
    idx_ref[...] = idx_out
    gate_ref[...] = gate_out / denom


def _router(x, g, w_r, b_r):
    t, d = x.shape
    tm = TOKEN_TILE
    tile = lambda w: pl.BlockSpec((tm, w), lambda i: (i, 0))
    return pl.pallas_call(
        _router_body,
        grid=(t // tm,),
        in_specs=[tile(d), _const_spec((1, d)), _const_spec(w_r.shape), _const_spec((1, N_EXPERTS))],
        out_specs=[pl.BlockSpec((tm * SUBLANES, LANES), lambda i: (i, 0)), tile(LANES), tile(LANES),
                   _const_spec((1, N_EXPERTS))],
        out_shape=[jax.ShapeDtypeStruct((t * SUBLANES, LANES), _F32), jax.ShapeDtypeStruct((t, LANES), jnp.int32),
                   jax.ShapeDtypeStruct((t, LANES), _F32), jax.ShapeDtypeStruct((1, N_EXPERTS), _F32)],
        scratch_shapes=[pltpu.VMEM((1, N_EXPERTS), _F32)],
        compiler_params=_params("arbitrary"),
        name="moe_router",
    )(x, g.reshape(1, d), w_r, b_r.reshape(1, N_EXPERTS))


def _dispatch_body(slot_ref, hp_ref, xb_in_ref, xb_ref, sem):
    del xb_in_ref
    tm = hp_ref.shape[0] // SUBLANES

    def start(r, carry):
        for k in range(TOP_K):
            s = slot_ref[r * TOP_K + k]
            pltpu.make_async_copy(_row_tile(hp_ref, r), _row_tile(xb_ref, s), sem).start(priority=k % 2)
        return carry

    lax.fori_loop(0, tm, start, 0, unroll=2)
    for _ in range(TOP_K):
        pltpu.make_async_copy(hp_ref, xb_ref.at[pl.ds(0, tm * SUBLANES)], sem).wait()


def _dispatch(hp, slot_flat, n_rows):
    t = hp.shape[0] // SUBLANES
    tm = TOKEN_TILE
    return pl.pallas_call(
        _dispatch_body,
        grid=(t // tm,),
        in_specs=[pl.BlockSpec((tm * TOP_K,), lambda i: (i,), memory_space=pltpu.SMEM),
                  pl.BlockSpec((tm * SUBLANES, LANES), lambda i: (i, 0)),
                  pl.BlockSpec(memory_space=pl.ANY)],
        out_specs=pl.BlockSpec(memory_space=pl.ANY),
        out_shape=jax.ShapeDtypeStruct((n_rows * SUBLANES, LANES), hp.dtype),
        scratch_shapes=[pltpu.SemaphoreType.DMA(())],
        input_output_aliases={2: 0},
        compiler_params=_params("arbitrary"),
        name="moe_dispatch",
    )(slot_flat, hp, jnp.zeros((n_rows * SUBLANES, LANES), hp.dtype))


def _expert_body(be_ref, nu_ref, xb_ref, w1_ref, b1g_ref, b1l_ref, w2_ref, b2_ref, perm_ref,
                 yb_ref, w1g_s, w1l_s, w2_s):
    b = pl.program_id(0)
    live = b < nu_ref[0]
    changed = (b == 0) | (be_ref[b] != be_ref[jnp.maximum(b - 1, 0)])

    @pl.when(live & changed)
    def _():
        group = perm_ref.shape[0]
        for c in range(w1_ref.shape[2] // group):
            wc = w1_ref[0, :, c * group:(c + 1) * group].astype(_BF16)
            d = _dot(wc, perm_ref[...]).astype(_BF16)
            w1g_s[:, c * (group // 2):(c + 1) * (group // 2)] = d[:, :group // 2]
            w1l_s[:, c * (group // 2):(c + 1) * (group // 2)] = d[:, group // 2:]
        w2_s[...] = w2_ref[0].astype(_BF16)

    @pl.when(live)
    def _():
        x = jnp.concatenate([_load_row_tile_column(xb_ref, MOE_BLOCK, j) for j in range(SUBLANES)],
                            axis=1).astype(_BF16)
        glu = _dot(x, w1g_s[...]) + b1g_ref[0]
        lin = _dot(x, w1l_s[...]) + b1l_ref[0]
        glu = jnp.minimum(glu, SWIGLU_LIMIT)
        lin = jnp.clip(lin, -SWIGLU_LIMIT, SWIGLU_LIMIT)
        act = glu * jax.nn.sigmoid(SWIGLU_ALPHA * glu) * (lin + 1.0)
        _store_row_tiles(yb_ref, _dot(act.astype(_BF16), w2_s[...]) + b2_ref[0])

    @pl.when(jnp.logical_not(live))
    def _():
        yb_ref[...] = jnp.zeros_like(yb_ref)


def _experts(xb, block_expert, n_used, layer, w1, b1, w2, b2):
    n_rows = xb.shape[0] // SUBLANES
    n_blocks = n_rows // MOE_BLOCK
    _, n_e, d, two_f = w1.shape
    row_spec = lambda idx: pl.BlockSpec((MOE_BLOCK * SUBLANES, LANES), lambda b, be, nu: (idx(b, be, nu), 0))
    f = two_f // 2
    group = 2 * LANES
    j = jnp.arange(group)
    src = jnp.where(j < LANES, 2 * j, 2 * (j - LANES) + 1)
    perm = (jnp.arange(group)[:, None] == src[None, :]).astype(_BF16)
    b1p = b1.reshape(n_e, 1, f, 2)
    blk = lambda b, be, nu: jnp.minimum(b, nu[0] - 1)
    e_spec = lambda shape: pl.BlockSpec((1,) + shape, lambda b, be, nu: (be[b],) + (0,) * len(shape))
    w_spec = lambda shape: pl.BlockSpec((None, 1) + shape, lambda b, be, nu: (layer, be[b], 0, 0))
    grid_spec = pltpu.PrefetchScalarGridSpec(
        num_scalar_prefetch=2,
        grid=(n_blocks,),
        in_specs=[row_spec(blk),
                  w_spec((d, two_f)), e_spec((1, f)), e_spec((1, f)), w_spec((f, d)), e_spec((1, d)),
                  pl.BlockSpec((group, group), lambda b, be, nu: (0, 0))],
        out_specs=row_spec(lambda b, be, nu: b),
        scratch_shapes=[pltpu.VMEM((d, f), _BF16), pltpu.VMEM((d, f), _BF16), pltpu.VMEM((f, d), _BF16)],
    )
    return pl.pallas_call(
        _expert_body,
        grid_spec=grid_spec,
        out_shape=jax.ShapeDtypeStruct((n_rows * SUBLANES, LANES), _F32),
        compiler_params=_params("arbitrary"),
        name="moe_experts",
    )(block_expert, n_used, xb, w1, b1p[..., 0], b1p[..., 1], w2, b2.reshape(n_e, 1, d), perm)


def _combine_body(slot_ref, next_slot_ref, x_ref, gate_ref, yb_ref, out_ref, rows_ref, sems):
    i = pl.program_id(0)
    tm = x_ref.shape[0]

    def gather(slots, p):
        def start(r, carry):
            for k in range(TOP_K):
                s = slots[r * TOP_K + k]
                pltpu.make_async_copy(_row_tile(yb_ref, s), _row_tile(rows_ref.at[p, k], r),
                                      sems.at[p]).start(priority=k % 2)
            return carry

        lax.fori_loop(0, tm, start, 0, unroll=2)

    def consume(p):
        for k in range(TOP_K):
            pltpu.make_async_copy(yb_ref.at[pl.ds(0, tm * SUBLANES)], rows_ref.at[p, k], sems.at[p]).wait()
        out_ref[...] = x_ref[...]
        for k in range(TOP_K):
            gate = jnp.broadcast_to(gate_ref[:, k:k + 1], (tm, LANES))
            for j in range(SUBLANES):
                cols = slice(j * LANES, (j + 1) * LANES)
                out_ref[:, cols] += gate * _load_row_tile_column(rows_ref, tm, j, (p, k))

    @pl.when(i == 0)
    def _():
        gather(slot_ref, 0)

    for p in range(2):
        @pl.when(i % 2 == p)
        def _():
            @pl.when(i + 1 < pl.num_programs(0))
            def _():
                gather(next_slot_ref, 1 - p)

            consume(p)


def _combine(x, gates, slot_flat, yb):
    t, d = x.shape
    tm = TOKEN_TILE
    last = t // tm - 1
    return pl.pallas_call(
        _combine_body,
        grid=(t // tm,),
        in_specs=[pl.BlockSpec((tm * TOP_K,), lambda i: (i,), memory_space=pltpu.SMEM),
                  pl.BlockSpec((tm * TOP_K,), lambda i: (jnp.minimum(i + 1, last),), memory_space=pltpu.SMEM),
                  pl.BlockSpec((tm, d), lambda i: (i, 0)),
                  pl.BlockSpec((tm, LANES), lambda i: (i, 0)),
                  pl.BlockSpec(memory_space=pl.ANY)],
        out_specs=pl.BlockSpec((tm, d), lambda i: (i, 0)),
        out_shape=jax.ShapeDtypeStruct((t, d), _F32),
        scratch_shapes=[pltpu.VMEM((2, TOP_K, tm * SUBLANES, LANES), _F32), pltpu.SemaphoreType.DMA((2,))],
        compiler_params=_params("arbitrary"),
        name="moe_combine",
    )(slot_flat, slot_flat, x, gates, yb)


def _moe(x, g, w_r, b_r, layer, w1, b1, w2, b2):
    t = x.shape[0]
    n_blocks = -(-(t * TOP_K + N_EXPERTS * (MOE_BLOCK - 1)) // MOE_BLOCK)
    hp, idx, gates, counts = _router(x, g, w_r, b_r)
    counts = counts[0].astype(jnp.int32)
    padded = (counts + MOE_BLOCK - 1) // MOE_BLOCK * MOE_BLOCK
    pad_end = jnp.cumsum(padded)
    pad_start = pad_end - padded
    experts = jnp.arange(N_EXPERTS, dtype=jnp.int32)
    start_of = jnp.sum(jnp.where(idx[:, :TOP_K, None] == experts, pad_start, 0), axis=-1)
    slot_flat = (start_of + idx[:, TOP_K:2 * TOP_K]).reshape(-1).astype(jnp.int32)
    n_used = (pad_end[-1] // MOE_BLOCK).astype(jnp.int32)
    block_id = jnp.minimum(jnp.arange(n_blocks, dtype=jnp.int32), n_used - 1)
    block_expert = jnp.sum(pad_end[None, :] <= (block_id * MOE_BLOCK)[:, None], axis=-1).astype(jnp.int32)
    xb = _dispatch(hp, slot_flat, n_blocks * MOE_BLOCK)
    yb = _experts(xb, block_expert, n_used.reshape(1), layer, w1, b1, w2, b2)
    return _combine(x, gates, slot_flat, yb)


def kernel(x, mix_norm, ffn_norm, even_w_in, even_w_s, even_b_s, even_g_v, even_g_q, even_g_k, even_w_out, odd_w_in, odd_g_cq, odd_g_ckv, odd_w_uq, odd_w_ukv, odd_g_q, odd_g_k, odd_w_out, router_w, router_b, expert_w1, expert_b1, expert_w2, expert_b2):
    bsz, s, d = x.shape
    xt = x.reshape(bsz * s, d)
    for l in range(mix_norm.shape[0]):
        i = l // 2
        if l % 2 == 0:
            a, qkv = _even_in(xt, mix_norm[l], even_w_in[i], even_w_s[i], even_b_s[i], even_g_v[i],
                              even_g_q[i], even_g_k[i], bsz)
            branches = [_dilated_branch(*qkv[bi], dil) for bi, (_, dil) in enumerate(B_BRANCHES)]
            xt = _even_out(xt, a, branches, even_w_out[i])
        else:
            q, k, v = _odd_in(xt, mix_norm[l], odd_w_in[i], odd_g_cq[i], odd_g_ckv[i], odd_w_uq[i],
                              odd_w_ukv[i], odd_g_q[i], odd_g_k[i], bsz)
            o = _flash(q, k, v, FLASH_TQ)
            xt = _proj_out(xt, o.reshape(bsz * s, -1), odd_w_out[i])
        xt = _moe(xt, ffn_norm[l], router_w[l], router_b[l], l, expert_w1, expert_b1[l],
                  expert_w2, expert_b2[l])
    return xt.reshape(bsz, s, d)
```

```python
import functools
import math

import jax
import jax.numpy as jnp
from jax import lax
from jax.experimental import pallas as pl
from jax.experimental.pallas import tpu as pltpu

EPS = 1e-6
NEG_INF = -1e30
A_HEADS = 4
A_HEAD_DIM = 128
A_CHUNK = 128
A_WIDTH = A_HEADS * A_HEAD_DIM
B_HEADS = 8
B_HEAD_DIM = 64
B_WIDTH = B_HEADS * B_HEAD_DIM
B_BRANCHES = ((128, 1), (512, 4), (2048, 16))
B_BLOCK = 128
C_HEADS = 8
C_NOPE = 128
C_ROPE = 64
C_V = 128
C_Q_LORA = 512
C_KV_LORA = 256
C_QK_PAD = 256
ROPE_THETA = 10000.0
N_EXPERTS = 32
TOP_K = 4
D_EXPERT = 1024
SWIGLU_ALPHA = 1.702
SWIGLU_LIMIT = 7.0
MOE_BLOCK = 512

LANES = 128
TOKEN_TILE = 256
VMEM_LIMIT = 56 * 1024 * 1024

_F32 = jnp.float32
_BF16 = jnp.bfloat16


def _params(*sem):
    return pltpu.CompilerParams(dimension_semantics=sem, vmem_limit_bytes=VMEM_LIMIT)


def _rms(x, g):
    return x * lax.rsqrt(jnp.mean(x * x, axis=-1, keepdims=True) + EPS) * g


def _gelu(x):
    return 0.5 * x * (1.0 + lax.erf(x * math.sqrt(0.5)))


def _dot(a, b):
    return jnp.dot(a, b, preferred_element_type=_F32)


def _dot_nt(a, b):
    return lax.dot_general(a, b, (((1,), (1,)), ((), ())), preferred_element_type=_F32)


def _seg_sum(x2, seg):
    hi = x2.astype(_BF16)
    lo = (x2 - hi.astype(_F32)).astype(_BF16)
    return _dot(hi, seg) + _dot(lo, seg)


def _const_spec(shape):
    nd = len(shape)
    return pl.BlockSpec(shape, lambda *_: (0,) * nd)


def _even_in_body(x_ref, g_ref, win_ref, ws_ref, bst_ref, gv_ref, gq_ref, gk_ref, seg_ref,
                  a_ref, *rest):
    qkv_refs, stage_ref = rest[:-1], rest[-1]
    x = x_ref[0]
    h = _rms(x, g_ref[...]).astype(_BF16)
    z = _dot(h, win_ref[...])
    tm = x.shape[0]
    u = _gelu(z[:, :A_WIDTH])
    vv = _gelu(z[:, A_WIDTH:2 * A_WIDTH])
    row = lax.broadcasted_iota(jnp.int32, (A_CHUNK, A_CHUNK), 0)
    col = lax.broadcasted_iota(jnp.int32, (A_CHUNK, A_CHUNK), 1)
    for hd in range(A_HEADS):
        sl = slice(hd * A_HEAD_DIM, (hd + 1) * A_HEAD_DIM)
        vh = _rms(vv[:, sl], gv_ref[hd:hd + 1, :]).astype(_BF16)
        w = jnp.where(row >= col, ws_ref[hd], 0.0).astype(_BF16)
        bias = bst_ref[:, hd:hd + 1]
        for c in range(tm // A_CHUNK):
            rs = slice(c * A_CHUNK, (c + 1) * A_CHUNK)
            mixed = _dot(w, vh[rs]) + bias
            a_ref[0, rs, sl] = (u[rs, sl] * mixed).astype(a_ref.dtype)
    o = 2 * A_WIDTH
    q = z[:, o:o + B_WIDTH]
    k = z[:, o + B_WIDTH:o + 2 * B_WIDTH]
    seg = seg_ref[...]
    inv = 1.0 / B_HEAD_DIM
    qn = q * lax.rsqrt(_seg_sum(q * q, seg) * inv + EPS) * gq_ref[...]
    kn = k * lax.rsqrt(_seg_sum(k * k, seg) * inv + EPS) * gk_ref[...]
    qkv = (qn * (B_HEAD_DIM ** -0.5), kn, z[:, o + 2 * B_WIDTH:])
    groups = B_WIDTH // LANES
    for which, val in enumerate(qkv):
        for gi in range(groups):
            stage_ref[gi] = val[:, gi * LANES:(gi + 1) * LANES]
        for bi, (_, dil) in enumerate(B_BRANCHES):
            ref = qkv_refs[3 * bi + which]
            if dil == 1:
                ref[0] = val.astype(ref.dtype)
                continue
            for r in range(dil):
                for gi in range(groups):
                    rows = stage_ref[gi, pl.ds(r, tm // dil, stride=dil), :]
                    lo = r * B_WIDTH + gi * LANES
                    ref[0, :, lo:lo + LANES] = rows.astype(ref.dtype)


def _class_view_spec(tm, dil):
    return pl.BlockSpec((1, tm // dil, dil * B_WIDTH), lambda b, j: (b, j, 0))


def _even_in(x, g, w_in, w_s, b_s, g_v, g_q, g_k, bsz):
    t, d = x.shape
    s = t // bsz
    tm = TOKEN_TILE
    seg = (jnp.arange(B_WIDTH)[:, None] // B_HEAD_DIM == jnp.arange(B_WIDTH)[None, :] // B_HEAD_DIM).astype(_BF16)
    tile = lambda w: pl.BlockSpec((1, tm, w), lambda b, j: (b, j, 0))
    qkv_specs, qkv_shapes = [], []
    for _, dil in B_BRANCHES:
        qkv_specs += [_class_view_spec(tm, dil)] * 3
        qkv_shapes += [jax.ShapeDtypeStruct((bsz, s // dil, dil * B_WIDTH), _BF16)] * 3
    outs = pl.pallas_call(
        _even_in_body,
        grid=(bsz, s // tm),
        in_specs=[tile(d), _const_spec((1, d)), _const_spec(w_in.shape), _const_spec(w_s.shape),
                  _const_spec((A_CHUNK, A_HEADS)), _const_spec(g_v.shape), _const_spec((1, B_WIDTH)),
                  _const_spec((1, B_WIDTH)), _const_spec(seg.shape)],
        out_specs=[tile(A_WIDTH)] + qkv_specs,
        out_shape=[jax.ShapeDtypeStruct((bsz, s, A_WIDTH), _BF16)] + qkv_shapes,
        scratch_shapes=[pltpu.VMEM((B_WIDTH // LANES, tm, LANES), _F32)],
        compiler_params=_params("parallel", "parallel"),
        name="even_in",
    )(x.reshape(bsz, s, d), g.reshape(1, d), w_in.astype(_BF16), w_s, b_s.T, g_v,
      jnp.tile(g_q, B_HEADS).reshape(1, B_WIDTH), jnp.tile(g_k, B_HEADS).reshape(1, B_WIDTH), seg)
    return outs[0], [outs[1 + 3 * bi:4 + 3 * bi] for bi in range(len(B_BRANCHES))]


def _dilated_body(*refs, blocks_per_class):
    nb = len(blocks_per_class)
    t = pl.program_id(1)
    for bi, n in enumerate(blocks_per_class):
        _dilated_block(*refs[5 * bi:5 * bi + 5], *refs[5 * nb + 2 * bi:5 * nb + 2 * bi + 2], t % n)


def _dilated_block(q_ref, kp_ref, kc_ref, vp_ref, vc_ref, o_ref, lse_ref, jb):
    q = q_ref[0]
    kk = jnp.concatenate([kp_ref[0], kc_ref[0]], axis=0)
    vv = jnp.concatenate([vp_ref[0], vc_ref[0]], axis=0)
    i = lax.broadcasted_iota(jnp.int32, (B_BLOCK, 2 * B_BLOCK), 0)
    c = lax.broadcasted_iota(jnp.int32, (B_BLOCK, 2 * B_BLOCK), 1)
    valid = (c >= i) & (c <= i + B_BLOCK) & ((jb > 0) | (c >= B_BLOCK))
    lane = lax.broadcasted_iota(jnp.int32, (1, LANES), 1)
    first = lane < B_HEAD_DIM
    for p in range(B_WIDTH // LANES):
        sl = slice(p * LANES, (p + 1) * LANES)
        q2, k2, v2 = q[:, sl], kk[:, sl], vv[:, sl]
        outs, lses = [], []
        for half in (first, jnp.logical_not(first)):
            s = _dot_nt(jnp.where(half, q2, jnp.zeros_like(q2)), k2)
            s = jnp.where(valid, s, NEG_INF)
            m = jnp.max(s, axis=-1, keepdims=True)
            e = jnp.exp(s - m)
            l = jnp.sum(e, axis=-1, keepdims=True)
            outs.append(_dot(e.astype(_BF16), v2) / l)
            lses.append(m + jnp.log(l))
        o_ref[0, :, sl] = jnp.where(first, outs[0], outs[1])
        lse_ref[0, :, sl] = jnp.where(first, lses[0], lses[1])


def _dilated_attention(qkv):
    bsz, s, _ = qkv[0][0].shape
    steps = s // B_BLOCK
    in_specs, out_specs, out_shapes, args, blocks_per_class = [], [], [], [], []
    for (q, k, v), (_, dil) in zip(qkv, B_BRANCHES):
        n = steps // dil
        cur = pl.BlockSpec((1, B_BLOCK, B_WIDTH), lambda b, t, n=n: (b, t % n, t // n))
        prev = pl.BlockSpec((1, B_BLOCK, B_WIDTH), lambda b, t, n=n: (b, jnp.maximum(t % n - 1, 0), t // n))
        in_specs += [cur, prev, cur, prev, cur]
        out_specs += [cur, cur]
        out_shapes += [jax.ShapeDtypeStruct(q.shape, _F32)] * 2
        args += [q, k, k, v, v]
        blocks_per_class.append(n)
    outs = pl.pallas_call(
        functools.partial(_dilated_body, blocks_per_class=tuple(blocks_per_class)),
        grid=(bsz, steps),
        in_specs=in_specs,
        out_specs=out_specs,
        out_shape=out_shapes,
        compiler_params=_params("parallel", "parallel"),
        name="dilated_attention",
    )(*args)
    return [(outs[2 * bi], outs[2 * bi + 1]) for bi in range(len(B_BRANCHES))]


def _even_out_body(x_ref, a_ref, *rest):
    nb = len(B_BRANCHES)
    branch_refs, w_ref, out_ref, stage_ref = rest[:2 * nb], rest[2 * nb], rest[2 * nb + 1], rest[2 * nb + 2]
    tm = x_ref.shape[1]

    def token_order(ref, dil, slot):
        if dil == 1:
            return ref[0]
        groups = B_WIDTH // LANES
        for r in range(dil):
            for gi in range(groups):
                lo = r * B_WIDTH + gi * LANES
                stage_ref[slot, gi, pl.ds(r, tm // dil, stride=dil), :] = ref[0, :, lo:lo + LANES]
        return jnp.concatenate([stage_ref[slot, gi] for gi in range(groups)], axis=1)

    outs = [token_order(branch_refs[2 * bi], dil, 2 * bi) for bi, (_, dil) in enumerate(B_BRANCHES)]
    lses = [token_order(branch_refs[2 * bi + 1], dil, 2 * bi + 1) for bi, (_, dil) in enumerate(B_BRANCHES)]
    m = functools.reduce(jnp.maximum, lses)
    es = [jnp.exp(l - m) for l in lses]
    b = sum(e * o for e, o in zip(es, outs)) / sum(es)
    y = _dot(a_ref[0], w_ref[:A_WIDTH, :]) + _dot(b.astype(_BF16), w_ref[A_WIDTH:, :])
    out_ref[0] = x_ref[0] + y


def _even_out(x, a, branches, w_out):
    t, d = x.shape
    bsz, s, _ = a.shape
    tm = TOKEN_TILE
    tile = lambda w: pl.BlockSpec((1, tm, w), lambda b, j: (b, j, 0))
    branch_specs = []
    for _, dil in B_BRANCHES:
        branch_specs += [_class_view_spec(tm, dil)] * 2
    out = pl.pallas_call(
        _even_out_body,
        grid=(bsz, s // tm),
        in_specs=[tile(d), tile(A_WIDTH)] + branch_specs + [_const_spec(w_out.shape)],
        out_specs=tile(d),
        out_shape=jax.ShapeDtypeStruct((bsz, s, d), _F32),
        scratch_shapes=[pltpu.VMEM((2 * len(B_BRANCHES), B_WIDTH // LANES, tm, LANES), _F32)],
        compiler_params=_params("parallel", "parallel"),
        name="even_out",
    )(x.reshape(bsz, s, d), a, *[arr for pair in branches for arr in pair], w_out.astype(_BF16))
    return out.reshape(t, d)


def _rope(x, cos, sin_signed, low):
    partner = jnp.where(low, pltpu.roll(x, LANES - C_ROPE // 2, 1), pltpu.roll(x, C_ROPE // 2, 1))
    return x * cos + partner * sin_signed


def _odd_in_body(x_ref, g_ref, win_ref, gcq_ref, gckv_ref, wqn_ref, wqr_ref, wkn_ref, wv_ref,
                 gqn_ref, gqr_ref, gkn_ref, gkr_ref, cos_ref, sin_ref, q_ref, k_ref, v_ref):
    x = x_ref[0]
    h = _rms(x, g_ref[...]).astype(_BF16)
    z = _dot(h, win_ref[...])
    cq = _rms(z[:, :C_Q_LORA], gcq_ref[...]).astype(_BF16)
    ckv = _rms(z[:, C_Q_LORA:C_Q_LORA + C_KV_LORA], gckv_ref[...]).astype(_BF16)
    kr = z[:, C_Q_LORA + C_KV_LORA:]
    qn = _dot(cq, wqn_ref[...])
    qr = _dot(cq, wqr_ref[...])
    kn = _dot(ckv, wkn_ref[...])
    vv = _dot(ckv, wv_ref[...])
    cos, sin = cos_ref[...], sin_ref[...]
    lane = lax.broadcasted_iota(jnp.int32, (1, LANES), 1)
    low = (lane % C_ROPE) < (C_ROPE // 2)
    inv = 1.0 / (C_NOPE + C_ROPE)
    scale = (C_NOPE + C_ROPE) ** -0.5 * math.log2(math.e)
    kr_ss = jnp.sum(kr * kr, axis=-1, keepdims=True)
    kr_roped = _rope(kr * gkr_ref[...], cos, sin, low)
    for hd in range(C_HEADS):
        sl = slice(hd * LANES, (hd + 1) * LANES)
        qnh, qrh, knh = qn[:, sl], qr[:, sl], kn[:, sl]
        ssq = jnp.sum(qnh * qnh, axis=-1, keepdims=True) + jnp.sum(qrh * qrh, axis=-1, keepdims=True)
        rq = lax.rsqrt(ssq * inv + EPS) * scale
        q_ref[0, hd, :, :LANES] = (qnh * rq * gqn_ref[...]).astype(q_ref.dtype)
        q_ref[0, hd, :, LANES:] = (_rope(qrh * gqr_ref[...], cos, sin, low) * rq).astype(q_ref.dtype)
        ssk = jnp.sum(knh * knh, axis=-1, keepdims=True) + kr_ss
        rk = lax.rsqrt(ssk * inv + EPS)
        k_ref[0, hd, :, :LANES] = (knh * rk * gkn_ref[...]).astype(k_ref.dtype)
        k_ref[0, hd, :, LANES:] = (kr_roped * rk).astype(k_ref.dtype)
        v_ref[0, hd, :, :C_V] = vv[:, sl].astype(v_ref.dtype)
        v_ref[0, hd, :, C_V:] = jnp.ones((x.shape[0], C_V), v_ref.dtype)


def _pad_lanes(a, width):
    return jnp.pad(a, [(0, 0)] * (a.ndim - 1) + [(0, width - a.shape[-1])])


def _odd_in(x, g, w_in, g_cq, g_ckv, w_uq, w_ukv, g_q, g_k, bsz):
    t, d = x.shape
    s = t // bsz
    tm = TOKEN_TILE
    w_in_p = _pad_lanes(w_in, C_Q_LORA + C_KV_LORA + LANES).astype(_BF16)
    wq = w_uq.reshape(C_Q_LORA, C_HEADS, C_NOPE + C_ROPE)
    wqn = wq[:, :, :C_NOPE].reshape(C_Q_LORA, C_HEADS * C_NOPE).astype(_BF16)
    wqr = _pad_lanes(wq[:, :, C_NOPE:], LANES).reshape(C_Q_LORA, C_HEADS * LANES).astype(_BF16)
    wkv = w_ukv.reshape(C_KV_LORA, C_HEADS, C_NOPE + C_V)
    wkn = wkv[:, :, :C_NOPE].reshape(C_KV_LORA, C_HEADS * C_NOPE).astype(_BF16)
    wv = wkv[:, :, C_NOPE:].reshape(C_KV_LORA, C_HEADS * C_V).astype(_BF16)
    half = C_ROPE // 2
    inv_freq = ROPE_THETA ** (-jnp.arange(half, dtype=_F32) / half)
    ang = jnp.arange(s, dtype=_F32)[:, None] * inv_freq[None, :]
    cos = jnp.tile(jnp.cos(ang), (1, LANES // half))
    sin = jnp.tile(jnp.concatenate([-jnp.sin(ang), jnp.sin(ang)], axis=-1), (1, LANES // C_ROPE))
    row = lambda a: a.reshape(1, -1)
    qk_shape = jax.ShapeDtypeStruct((bsz, C_HEADS, s, C_QK_PAD), _BF16)
    head_spec = lambda w: pl.BlockSpec((1, C_HEADS, tm, w), lambda b, j: (b, 0, j, 0))
    pos_spec = pl.BlockSpec((tm, LANES), lambda b, j: (j, 0))
    consts = [row(g), w_in_p, row(g_cq), row(g_ckv), wqn, wqr, wkn, wv,
              row(g_q[:C_NOPE]), row(_pad_lanes(g_q[C_NOPE:], LANES)),
              row(g_k[:C_NOPE]), row(_pad_lanes(g_k[C_NOPE:], LANES))]
    return pl.pallas_call(
        _odd_in_body,
        grid=(bsz, s // tm),
        in_specs=[pl.BlockSpec((1, tm, d), lambda b, j: (b, j, 0))]
        + [_const_spec(c.shape) for c in consts] + [pos_spec, pos_spec],
        out_specs=[head_spec(C_QK_PAD), head_spec(C_QK_PAD), head_spec(2 * C_V)],
        out_shape=[qk_shape, qk_shape, jax.ShapeDtypeStruct((bsz, C_HEADS, s, 2 * C_V), _BF16)],
        compiler_params=_params("parallel", "parallel"),
        name="odd_in",
    )(x.reshape(bsz, s, d), *consts, cos, sin)


FLASH_ROWS = 128
FLASH_TQ = 1024


def _flash_body(q_ref, k_ref, v_ref, o_ref, s_ref, m_ref, acc_ref):
    qi = pl.program_id(2)
    tq = q_ref.shape[2]
    tk = tq
    dv = o_ref.shape[2]
    m_ref[...] = jnp.full_like(m_ref, NEG_INF)
    acc_ref[...] = jnp.zeros_like(acc_ref)

    def scores(j):
        start = pl.multiple_of(j * tk, tk)
        return _dot_nt(q_ref[0, 0], k_ref[0, 0, pl.ds(start, tk), :])

    def softmax_pv(s_all, j, diagonal):
        start = pl.multiple_of(j * tk, tk)
        v = v_ref[0, 0, pl.ds(start, tk), :]
        if diagonal:
            r = lax.broadcasted_iota(jnp.int32, s_all.shape, 0)
            col = lax.broadcasted_iota(jnp.int32, s_all.shape, 1)
            s_all = jnp.where(col <= r, s_all, NEG_INF)
        ps, scales = [], []
        for c in range(tq // FLASH_ROWS):
            rows = slice(c * FLASH_ROWS, (c + 1) * FLASH_ROWS)
            groups = [s_all[rows, g * LANES:(g + 1) * LANES] for g in range(tk // LANES)]
            m_old = m_ref[rows]
            gmax = functools.reduce(jnp.maximum, groups)
            m_new = jnp.maximum(m_old, jnp.max(gmax, axis=-1, keepdims=True))
            alpha = jnp.exp2(m_old - m_new)
            ps.append(jnp.concatenate([jnp.exp2((g - m_new).astype(_BF16)) for g in groups], axis=1))
            scales.append(jnp.concatenate([alpha] * (acc_ref.shape[1] // LANES), axis=1))
            m_ref[rows] = m_new
        pv = _dot(jnp.concatenate(ps, axis=0), v)
        acc_ref[...] = jnp.concatenate(scales, axis=0) * acc_ref[...] + pv

    s_ref[...] = scores(0)

    def full_tile(j, carry):
        s_cur = s_ref[...]
        s_next = scores(j + 1)
        softmax_pv(s_cur, j, False)
        s_ref[...] = s_next
        return carry

    lax.fori_loop(0, qi, full_tile, 0)
    softmax_pv(s_ref[...], qi, True)
    acc = acc_ref[...]
    o_ref[0] = (acc[:, :dv] / acc[:, dv:2 * dv]).astype(o_ref.dtype)


def _flash(q, k, v, tq):
    bsz, nh, s, dq = q.shape
    dv2 = v.shape[-1]
    dv = dv2 // 2
    whole = lambda w: pl.BlockSpec((1, 1, s, w), lambda b, h, i: (b, h, 0, 0))
    return pl.pallas_call(
        _flash_body,
        grid=(bsz, nh, s // tq),
        in_specs=[pl.BlockSpec((1, 1, tq, dq), lambda b, h, i: (b, h, i, 0)), whole(dq), whole(dv2)],
        out_specs=pl.BlockSpec((1, tq, dv), lambda b, h, i: (b, i, h)),
        out_shape=jax.ShapeDtypeStruct((bsz, s, nh * dv), _BF16),
        scratch_shapes=[pltpu.VMEM((tq, tq), _F32), pltpu.VMEM((tq, LANES), _F32), pltpu.VMEM((tq, dv2), _F32)],
        compiler_params=_params("parallel", "parallel", "arbitrary"),
        name="mla_flash",
    )(q, k, v)


def _proj_out_body(x_ref, o_ref, w_ref, out_ref):
    out_ref[...] = x_ref[...] + _dot(o_ref[...], w_ref[...])


def _proj_out(x, o, w_out):
    t, d = x.shape
    tm = TOKEN_TILE
    tile = lambda w: pl.BlockSpec((tm, w), lambda i: (i, 0))
    return pl.pallas_call(
        _proj_out_body,
        grid=(t // tm,),
        in_specs=[tile(d), tile(o.shape[1]), _const_spec(w_out.shape)],
        out_specs=tile(d),
        out_shape=jax.ShapeDtypeStruct((t, d), _F32),
        compiler_params=_params("parallel"),
        name="odd_out",
    )(x, o, w_out.astype(_BF16))


SUBLANES = 8


def _store_row_tiles(ref, val, lead=()):
    n = val.shape[0]
    for j in range(SUBLANES):
        ref[lead + (pl.ds(j, n, stride=SUBLANES), slice(None))] = val[:, j * LANES:(j + 1) * LANES]


def _load_row_tile_column(ref, n, j, lead=()):
    return ref[lead + (pl.ds(j, n, stride=SUBLANES), slice(None))]


def _row_tile(ref, row):
    return ref.at[pl.ds(pl.multiple_of(row * SUBLANES, SUBLANES), SUBLANES)]


def _router_body(x_ref, g_ref, w_ref, b_ref, hp_ref, idx_ref, gate_ref, cnt_ref, base_ref):
    @pl.when(pl.program_id(0) == 0)
    def _():
        base_ref[...] = jnp.zeros_like(base_ref)

    h = _rms(x_ref[...], g_ref[...])
    tm, d = h.shape
    w = w_ref[...]
    h_hi, w_hi = h.astype(_BF16), w.astype(_BF16)
    h_lo = (h - h_hi.astype(_F32)).astype(_BF16)
    w_lo = (w - w_hi.astype(_F32)).astype(_BF16)
    logits = _dot(h_hi, w_hi) + _dot(h_lo, w_hi) + _dot(h_hi, w_lo) + b_ref[...]
    _store_row_tiles(hp_ref, h)

    col = lax.broadcasted_iota(jnp.int32, logits.shape, 1)
    lane = lax.broadcasted_iota(jnp.int32, (tm, LANES), 1)
    r = lax.broadcasted_iota(jnp.int32, (tm, tm), 0)
    c = lax.broadcasted_iota(jnp.int32, (tm, tm), 1)
    before = (c < r).astype(_BF16)
    base = base_ref[...]
    idx_out = jnp.zeros((tm, LANES), jnp.int32)
    gate_out = jnp.zeros((tm, LANES), _F32)
    top0 = None
    denom = None
    work = logits
    for k in range(TOP_K):
        m = jnp.max(work, axis=-1, keepdims=True)
        sel = jnp.min(jnp.where(work == m, col, N_EXPERTS), axis=-1, keepdims=True)
        hit = col == sel
        work = jnp.where(hit, -jnp.inf, work)
        onehot = hit.astype(_F32)
        earlier = _dot(before, hit.astype(_BF16)) + base
        rank = jnp.sum(onehot * earlier, axis=-1, keepdims=True).astype(jnp.int32)
        base = base + jnp.sum(onehot, axis=0, keepdims=True)
        if k == 0:
            top0 = m
            e = jnp.ones_like(m)
            denom = e
        else:
            e = jnp.exp(m - top0)
            denom = denom + e
        idx_out = jnp.where(lane == k, sel, idx_out)
        idx_out = jnp.where(lane == TOP_K + k, rank, idx_out)
        gate_out = jnp.where(lane == k, e, gate_out)
    base_ref[...] = base
    cnt_ref[...] = base
    idx_ref[...] = idx_out
    gate_ref[...] = gate_out / denom


def _router(x, g, w_r, b_r):
    t, d = x.shape
    tm = TOKEN_TILE
    tile = lambda w: pl.BlockSpec((tm, w), lambda i: (i, 0))
    return pl.pallas_call(
        _router_body,
        grid=(t // tm,),
        in_specs=[tile(d), _const_spec((1, d)), _const_spec(w_r.shape), _const_spec((1, N_EXPERTS))],
        out_specs=[pl.BlockSpec((tm * SUBLANES, LANES), lambda i: (i, 0)), tile(LANES), tile(LANES),
                   _const_spec((1, N_EXPERTS))],
        out_shape=[jax.ShapeDtypeStruct((t * SUBLANES, LANES), _F32), jax.ShapeDtypeStruct((t, LANES), jnp.int32),
                   jax.ShapeDtypeStruct((t, LANES), _F32), jax.ShapeDtypeStruct((1, N_EXPERTS), _F32)],
        scratch_shapes=[pltpu.VMEM((1, N_EXPERTS), _F32)],
        compiler_params=_params("arbitrary"),
        name="moe_router",
    )(x, g.reshape(1, d), w_r, b_r.reshape(1, N_EXPERTS))


def _dispatch_body(slot_ref, hp_ref, xb_in_ref, xb_ref, sem):
    del xb_in_ref
    tm = hp_ref.shape[0] // SUBLANES

    def start(r, carry):
        for k in range(TOP_K):
            s = slot_ref[r * TOP_K + k]
            pltpu.make_async_copy(_row_tile(hp_ref, r), _row_tile(xb_ref, s), sem).start(priority=k % 2)
        return carry

    lax.fori_loop(0, tm, start, 0, unroll=2)
    for _ in range(TOP_K):
        pltpu.make_async_copy(hp_ref, xb_ref.at[pl.ds(0, tm * SUBLANES)], sem).wait()


def _dispatch(hp, slot_flat, xb_init):
    t = hp.shape[0] // SUBLANES
    tm = TOKEN_TILE
    return pl.pallas_call(
        _dispatch_body,
        grid=(t // tm,),
        in_specs=[pl.BlockSpec((tm * TOP_K,), lambda i: (i,), memory_space=pltpu.SMEM),
                  pl.BlockSpec((tm * SUBLANES, LANES), lambda i: (i, 0)),
                  pl.BlockSpec(memory_space=pl.ANY)],
        out_specs=pl.BlockSpec(memory_space=pl.ANY),
        out_shape=jax.ShapeDtypeStruct(xb_init.shape, hp.dtype),
        scratch_shapes=[pltpu.SemaphoreType.DMA(())],
        input_output_aliases={2: 0},
        compiler_params=_params("arbitrary"),
        name="moe_dispatch",
    )(slot_flat, hp, xb_init)


def _expert_body(be_ref, nu_ref, xb_ref, w1_ref, b1g_ref, b1l_ref, w2_ref, b2_ref, perm_ref,
                 yb_ref, w1g_s, w1l_s, w2_s):
    b = pl.program_id(0)
    live = b < nu_ref[0]
    changed = (b == 0) | (be_ref[b] != be_ref[jnp.maximum(b - 1, 0)])

    @pl.when(live & changed)
    def _():
        group = perm_ref.shape[0]
        for c in range(w1_ref.shape[2] // group):
            wc = w1_ref[0, :, c * group:(c + 1) * group].astype(_BF16)
            d = _dot(wc, perm_ref[...]).astype(_BF16)
            w1g_s[:, c * (group // 2):(c + 1) * (group // 2)] = d[:, :group // 2]
            w1l_s[:, c * (group // 2):(c + 1) * (group // 2)] = d[:, group // 2:]
        w2_s[...] = w2_ref[0].astype(_BF16)

    @pl.when(live)
    def _():
        x = jnp.concatenate([_load_row_tile_column(xb_ref, MOE_BLOCK, j) for j in range(SUBLANES)],
                            axis=1).astype(_BF16)
        glu = _dot(x, w1g_s[...]) + b1g_ref[0]
        lin = _dot(x, w1l_s[...]) + b1l_ref[0]
        glu = jnp.minimum(glu, SWIGLU_LIMIT)
        lin = jnp.clip(lin, -SWIGLU_LIMIT, SWIGLU_LIMIT)
        act = glu * jax.nn.sigmoid(SWIGLU_ALPHA * glu) * (lin + 1.0)
        _store_row_tiles(yb_ref, _dot(act.astype(_BF16), w2_s[...]) + b2_ref[0])

    @pl.when(jnp.logical_not(live))
    def _():
        yb_ref[...] = jnp.zeros_like(yb_ref)


def _experts(xb, block_expert, n_used, layer, w1, b1, w2, b2):
    n_rows = xb.shape[0] // SUBLANES
    n_blocks = n_rows // MOE_BLOCK
    _, n_e, d, two_f = w1.shape
    row_spec = lambda idx: pl.BlockSpec((MOE_BLOCK * SUBLANES, LANES), lambda b, be, nu: (idx(b, be, nu), 0))
    f = two_f // 2
    group = 2 * LANES
    j = jnp.arange(group)
    src = jnp.where(j < LANES, 2 * j, 2 * (j - LANES) + 1)
    perm = (jnp.arange(group)[:, None] == src[None, :]).astype(_BF16)
    b1p = b1.reshape(n_e, 1, f, 2)
    blk = lambda b, be, nu: jnp.minimum(b, nu[0] - 1)
    e_spec = lambda shape: pl.BlockSpec((1,) + shape, lambda b, be, nu: (be[b],) + (0,) * len(shape))
    w_spec = lambda shape: pl.BlockSpec((None, 1) + shape, lambda b, be, nu: (layer, be[b], 0, 0))
    grid_spec = pltpu.PrefetchScalarGridSpec(
        num_scalar_prefetch=2,
        grid=(n_blocks,),
        in_specs=[row_spec(blk),
                  w_spec((d, two_f)), e_spec((1, f)), e_spec((1, f)), w_spec((f, d)), e_spec((1, d)),
                  pl.BlockSpec((group, group), lambda b, be, nu: (0, 0))],
        out_specs=row_spec(lambda b, be, nu: b),
        scratch_shapes=[pltpu.VMEM((d, f), _BF16), pltpu.VMEM((d, f), _BF16), pltpu.VMEM((f, d), _BF16)],
    )
    return pl.pallas_call(
        _expert_body,
        grid_spec=grid_spec,
        out_shape=jax.ShapeDtypeStruct((n_rows * SUBLANES, LANES), _F32),
        compiler_params=_params("arbitrary"),
        name="moe_experts",
    )(block_expert, n_used, xb, w1, b1p[..., 0], b1p[..., 1], w2, b2.reshape(n_e, 1, d), perm)


def _combine_body(slot_ref, next_slot_ref, x_ref, gate_ref, yb_ref, out_ref, rows_ref, sems):
    i = pl.program_id(0)
    tm = x_ref.shape[0]

    def gather(slots, p):
        def start(r, carry):
            for k in range(TOP_K):
                s = slots[r * TOP_K + k]
                pltpu.make_async_copy(_row_tile(yb_ref, s), _row_tile(rows_ref.at[p, k], r),
                                      sems.at[p]).start(priority=k % 2)
            return carry

        lax.fori_loop(0, tm, start, 0, unroll=2)

    def consume(p):
        for k in range(TOP_K):
            pltpu.make_async_copy(yb_ref.at[pl.ds(0, tm * SUBLANES)], rows_ref.at[p, k], sems.at[p]).wait()
        out_ref[...] = x_ref[...]
        for k in range(TOP_K):
            gate = jnp.broadcast_to(gate_ref[:, k:k + 1], (tm, LANES))
            for j in range(SUBLANES):
                cols = slice(j * LANES, (j + 1) * LANES)
                out_ref[:, cols] += gate * _load_row_tile_column(rows_ref, tm, j, (p, k))

    @pl.when(i == 0)
    def _():
        gather(slot_ref, 0)

    for p in range(2):
        @pl.when(i % 2 == p)
        def _():
            @pl.when(i + 1 < pl.num_programs(0))
            def _():
                gather(next_slot_ref, 1 - p)

            consume(p)


def _combine(x, gates, slot_flat, yb):
    t, d = x.shape
    tm = TOKEN_TILE
    last = t // tm - 1
    return pl.pallas_call(
        _combine_body,
        grid=(t // tm,),
        in_specs=[pl.BlockSpec((tm * TOP_K,), lambda i: (i,), memory_space=pltpu.SMEM),
                  pl.BlockSpec((tm * TOP_K,), lambda i: (jnp.minimum(i + 1, last),), memory_space=pltpu.SMEM),
                  pl.BlockSpec((tm, d), lambda i: (i, 0)),
                  pl.BlockSpec((tm, LANES), lambda i: (i, 0)),
                  pl.BlockSpec(memory_space=pl.ANY)],
        out_specs=pl.BlockSpec((tm, d), lambda i: (i, 0)),
        out_shape=jax.ShapeDtypeStruct((t, d), _F32),
        scratch_shapes=[pltpu.VMEM((2, TOP_K, tm * SUBLANES, LANES), _F32), pltpu.SemaphoreType.DMA((2,))],
        compiler_params=_params("arbitrary"),
        name="moe_combine",
    )(slot_flat, slot_flat, x, gates, yb)


def _moe_blocks(t):
    return -(-(t * TOP_K + N_EXPERTS * (MOE_BLOCK - 1)) // MOE_BLOCK)


def _moe(x, g, w_r, b_r, layer, w1, b1, w2, b2, xb_init):
    t = x.shape[0]
    n_blocks = _moe_blocks(t)
    hp, idx, gates, counts = _router(x, g, w_r, b_r)
    counts = counts[0].astype(jnp.int32)
    padded = (counts + MOE_BLOCK - 1) // MOE_BLOCK * MOE_BLOCK
    pad_end = jnp.cumsum(padded)
    pad_start = pad_end - padded
    experts = jnp.arange(N_EXPERTS, dtype=jnp.int32)
    start_of = jnp.sum(jnp.where(idx[:, :TOP_K, None] == experts, pad_start, 0), axis=-1)
    slot_flat = (start_of + idx[:, TOP_K:2 * TOP_K]).reshape(-1).astype(jnp.int32)
    n_used = (pad_end[-1] // MOE_BLOCK).astype(jnp.int32)
    block_id = jnp.minimum(jnp.arange(n_blocks, dtype=jnp.int32), n_used - 1)
    block_expert = jnp.sum(pad_end[None, :] <= (block_id * MOE_BLOCK)[:, None], axis=-1).astype(jnp.int32)
    xb = _dispatch(hp, slot_flat, xb_init)
    yb = _experts(xb, block_expert, n_used.reshape(1), layer, w1, b1, w2, b2)
    return _combine(x, gates, slot_flat, yb), xb


def kernel(x, mix_norm, ffn_norm, even_w_in, even_w_s, even_b_s, even_g_v, even_g_q, even_g_k, even_w_out, odd_w_in, odd_g_cq, odd_g_ckv, odd_w_uq, odd_w_ukv, odd_g_q, odd_g_k, odd_w_out, router_w, router_b, expert_w1, expert_b1, expert_w2, expert_b2):
    bsz, s, d = x.shape
    xt = x.reshape(bsz * s, d)
    xb = jnp.zeros((_moe_blocks(bsz * s) * MOE_BLOCK * SUBLANES, LANES), _F32)
    for l in range(mix_norm.shape[0]):
        i = l // 2
        if l % 2 == 0:
            a, qkv = _even_in(xt, mix_norm[l], even_w_in[i], even_w_s[i], even_b_s[i], even_g_v[i],
                              even_g_q[i], even_g_k[i], bsz)
            xt = _even_out(xt, a, _dilated_attention(qkv), even_w_out[i])
        else:
            q, k, v = _odd_in(xt, mix_norm[l], odd_w_in[i], odd_g_cq[i], odd_g_ckv[i], odd_w_uq[i],
                              odd_w_ukv[i], odd_g_q[i], odd_g_k[i], bsz)
            o = _flash(q, k, v, FLASH_TQ)
            xt = _proj_out(xt, o.reshape(bsz * s, -1), odd_w_out[i])
        xt, xb = _moe(xt, ffn_norm[l], router_w[l], router_b[l], l, expert_w1, expert_b1[l],
                      expert_w2, expert_b2[l], xb)
    return xt.reshape(bsz, s, d)
```

```python
import functools
import math

import jax
import jax.numpy as jnp
from jax import lax
from jax.experimental import pallas as pl
from jax.experimental.pallas import tpu as pltpu

EPS = 1e-6
NEG_INF = -1e30
A_HEADS = 4
A_HEAD_DIM = 128
A_CHUNK = 128
A_WIDTH = A_HEADS * A_HEAD_DIM
B_HEADS = 8
B_HEAD_DIM = 64
B_WIDTH = B_HEADS * B_HEAD_DIM
B_BRANCHES = ((128, 1), (512, 4), (2048, 16))
B_BLOCK = 128
C_HEADS = 8
C_NOPE = 128
C_ROPE = 64
C_V = 128
C_Q_LORA = 512
C_KV_LORA = 256
C_QK_PAD = 256
ROPE_THETA = 10000.0
N_EXPERTS = 32
TOP_K = 4
D_EXPERT = 1024
SWIGLU_ALPHA = 1.702
SWIGLU_LIMIT = 7.0
MOE_BLOCK = 512

LANES = 128
TOKEN_TILE = 256
VMEM_LIMIT = 56 * 1024 * 1024

_F32 = jnp.float32
_BF16 = jnp.bfloat16


def _params(*sem):
    return pltpu.CompilerParams(dimension_semantics=sem, vmem_limit_bytes=VMEM_LIMIT)


def _rms(x, g):
    return x * lax.rsqrt(jnp.mean(x * x, axis=-1, keepdims=True) + EPS) * g


def _gelu(x):
    return 0.5 * x * (1.0 + lax.erf(x * math.sqrt(0.5)))


def _dot(a, b):
    return jnp.dot(a, b, preferred_element_type=_F32)


def _dot_nt(a, b):
    return lax.dot_general(a, b, (((1,), (1,)), ((), ())), preferred_element_type=_F32)


def _seg_sum(x2, seg):
    hi = x2.astype(_BF16)
    lo = (x2 - hi.astype(_F32)).astype(_BF16)
    return _dot(hi, seg) + _dot(lo, seg)


def _const_spec(shape):
    nd = len(shape)
    return pl.BlockSpec(shape, lambda *_: (0,) * nd)


def _even_in_body(x_ref, g_ref, win_ref, ws_ref, bst_ref, gv_ref, gq_ref, gk_ref, seg_ref,
                  a_ref, *rest):
    qkv_refs, stage_ref = rest[:-1], rest[-1]
    x = x_ref[0]
    h = _rms(x, g_ref[...]).astype(_BF16)
    z = _dot(h, win_ref[...])
    tm = x.shape[0]
    u = _gelu(z[:, :A_WIDTH])
    vv = _gelu(z[:, A_WIDTH:2 * A_WIDTH])
    row = lax.broadcasted_iota(jnp.int32, (A_CHUNK, A_CHUNK), 0)
    col = lax.broadcasted_iota(jnp.int32, (A_CHUNK, A_CHUNK), 1)
    for hd in range(A_HEADS):
        sl = slice(hd * A_HEAD_DIM, (hd + 1) * A_HEAD_DIM)
        vh = _rms(vv[:, sl], gv_ref[hd:hd + 1, :]).astype(_BF16)
        w = jnp.where(row >= col, ws_ref[hd], 0.0).astype(_BF16)
        bias = bst_ref[:, hd:hd + 1]
        for c in range(tm // A_CHUNK):
            rs = slice(c * A_CHUNK, (c + 1) * A_CHUNK)
            mixed = _dot(w, vh[rs]) + bias
            a_ref[0, rs, sl] = (u[rs, sl] * mixed).astype(a_ref.dtype)
    o = 2 * A_WIDTH
    q = z[:, o:o + B_WIDTH]
    k = z[:, o + B_WIDTH:o + 2 * B_WIDTH]
    seg = seg_ref[...]
    inv = 1.0 / B_HEAD_DIM
    qn = q * lax.rsqrt(_seg_sum(q * q, seg) * inv + EPS) * gq_ref[...]
    kn = k * lax.rsqrt(_seg_sum(k * k, seg) * inv + EPS) * gk_ref[...]
    qkv = (qn * (B_HEAD_DIM ** -0.5), kn, z[:, o + 2 * B_WIDTH:])
    groups = B_WIDTH // LANES
    for which, val in enumerate(qkv):
        for gi in range(groups):
            stage_ref[gi] = val[:, gi * LANES:(gi + 1) * LANES]
        for bi, (_, dil) in enumerate(B_BRANCHES):
            ref = qkv_refs[3 * bi + which]
            if dil == 1:
                ref[0] = val.astype(ref.dtype)
                continue
            for r in range(dil):
                for gi in range(groups):
                    rows = stage_ref[gi, pl.ds(r, tm // dil, stride=dil), :]
                    lo = r * B_WIDTH + gi * LANES
                    ref[0, :, lo:lo + LANES] = rows.astype(ref.dtype)


def _class_view_spec(tm, dil):
    return pl.BlockSpec((1, tm // dil, dil * B_WIDTH), lambda b, j: (b, j, 0))


def _even_in(x, g, w_in, w_s, b_s, g_v, g_q, g_k, bsz):
    t, d = x.shape
    s = t // bsz
    tm = TOKEN_TILE
    seg = (jnp.arange(B_WIDTH)[:, None] // B_HEAD_DIM == jnp.arange(B_WIDTH)[None, :] // B_HEAD_DIM).astype(_BF16)
    tile = lambda w: pl.BlockSpec((1, tm, w), lambda b, j: (b, j, 0))
    qkv_specs, qkv_shapes = [], []
    for _, dil in B_BRANCHES:
        qkv_specs += [_class_view_spec(tm, dil)] * 3
        qkv_shapes += [jax.ShapeDtypeStruct((bsz, s // dil, dil * B_WIDTH), _BF16)] * 3
    outs = pl.pallas_call(
        _even_in_body,
        grid=(bsz, s // tm),
        in_specs=[tile(d), _const_spec((1, d)), _const_spec(w_in.shape), _const_spec(w_s.shape),
                  _const_spec((A_CHUNK, A_HEADS)), _const_spec(g_v.shape), _const_spec((1, B_WIDTH)),
                  _const_spec((1, B_WIDTH)), _const_spec(seg.shape)],
        out_specs=[tile(A_WIDTH)] + qkv_specs,
        out_shape=[jax.ShapeDtypeStruct((bsz, s, A_WIDTH), _BF16)] + qkv_shapes,
        scratch_shapes=[pltpu.VMEM((B_WIDTH // LANES, tm, LANES), _F32)],
        compiler_params=_params("parallel", "parallel"),
        name="even_in",
    )(x.reshape(bsz, s, d), g.reshape(1, d), w_in.astype(_BF16), w_s, b_s.T, g_v,
      jnp.tile(g_q, B_HEADS).reshape(1, B_WIDTH), jnp.tile(g_k, B_HEADS).reshape(1, B_WIDTH), seg)
    return outs[0], [outs[1 + 3 * bi:4 + 3 * bi] for bi in range(len(B_BRANCHES))]


def _dilated_body(*refs, blocks_per_class):
    nb = len(blocks_per_class)
    t = pl.program_id(1)
    for bi, n in enumerate(blocks_per_class):
        _dilated_block(*refs[5 * bi:5 * bi + 5], *refs[5 * nb + 2 * bi:5 * nb + 2 * bi + 2], t % n)


def _dilated_block(q_ref, kp_ref, kc_ref, vp_ref, vc_ref, o_ref, lse_ref, jb):
    q = q_ref[0]
    kk = jnp.concatenate([kp_ref[0], kc_ref[0]], axis=0)
    vv = jnp.concatenate([vp_ref[0], vc_ref[0]], axis=0)
    i = lax.broadcasted_iota(jnp.int32, (B_BLOCK, 2 * B_BLOCK), 0)
    c = lax.broadcasted_iota(jnp.int32, (B_BLOCK, 2 * B_BLOCK), 1)
    valid = (c >= i) & (c <= i + B_BLOCK) & ((jb > 0) | (c >= B_BLOCK))
    lane = lax.broadcasted_iota(jnp.int32, (1, LANES), 1)
    first = lane < B_HEAD_DIM
    for p in range(B_WIDTH // LANES):
        sl = slice(p * LANES, (p + 1) * LANES)
        q2, k2, v2 = q[:, sl], kk[:, sl], vv[:, sl]
        outs, lses = [], []
        for half in (first, jnp.logical_not(first)):
            s = _dot_nt(jnp.where(half, q2, jnp.zeros_like(q2)), k2)
            s = jnp.where(valid, s, NEG_INF)
            m = jnp.max(s, axis=-1, keepdims=True)
            e = jnp.exp(s - m)
            l = jnp.sum(e, axis=-1, keepdims=True)
            outs.append(_dot(e.astype(_BF16), v2) / l)
            lses.append(m + jnp.log(l))
        o_ref[0, :, sl] = jnp.where(first, outs[0], outs[1])
        lse_ref[0, :, sl] = jnp.where(first, lses[0], lses[1])


def _dilated_attention(qkv):
    bsz, s, _ = qkv[0][0].shape
    steps = s // B_BLOCK
    in_specs, out_specs, out_shapes, args, blocks_per_class = [], [], [], [], []
    for (q, k, v), (_, dil) in zip(qkv, B_BRANCHES):
        n = steps // dil
        cur = pl.BlockSpec((1, B_BLOCK, B_WIDTH), lambda b, t, n=n: (b, t % n, t // n))
        prev = pl.BlockSpec((1, B_BLOCK, B_WIDTH), lambda b, t, n=n: (b, jnp.maximum(t % n - 1, 0), t // n))
        in_specs += [cur, prev, cur, prev, cur]
        out_specs += [cur, cur]
        out_shapes += [jax.ShapeDtypeStruct(q.shape, _F32)] * 2
        args += [q, k, k, v, v]
        blocks_per_class.append(n)
    outs = pl.pallas_call(
        functools.partial(_dilated_body, blocks_per_class=tuple(blocks_per_class)),
        grid=(bsz, steps),
        in_specs=in_specs,
        out_specs=out_specs,
        out_shape=out_shapes,
        compiler_params=_params("parallel", "parallel"),
        name="dilated_attention",
    )(*args)
    return [(outs[2 * bi], outs[2 * bi + 1]) for bi in range(len(B_BRANCHES))]


def _even_out_body(x_ref, a_ref, *rest):
    nb = len(B_BRANCHES)
    branch_refs, w_ref, out_ref, stage_ref = rest[:2 * nb], rest[2 * nb], rest[2 * nb + 1], rest[2 * nb + 2]
    tm = x_ref.shape[1]

    def token_order(ref, dil, slot):
        if dil == 1:
            return ref[0]
        groups = B_WIDTH // LANES
        for r in range(dil):
            for gi in range(groups):
                lo = r * B_WIDTH + gi * LANES
                stage_ref[slot, gi, pl.ds(r, tm // dil, stride=dil), :] = ref[0, :, lo:lo + LANES]
        return jnp.concatenate([stage_ref[slot, gi] for gi in range(groups)], axis=1)

    outs = [token_order(branch_refs[2 * bi], dil, 2 * bi) for bi, (_, dil) in enumerate(B_BRANCHES)]
    lses = [token_order(branch_refs[2 * bi + 1], dil, 2 * bi + 1) for bi, (_, dil) in enumerate(B_BRANCHES)]
    m = functools.reduce(jnp.maximum, lses)
    es = [jnp.exp(l - m) for l in lses]
    b = sum(e * o for e, o in zip(es, outs)) / sum(es)
    y = _dot(a_ref[0], w_ref[:A_WIDTH, :]) + _dot(b.astype(_BF16), w_ref[A_WIDTH:, :])
    out_ref[0] = x_ref[0] + y


def _even_out(x, a, branches, w_out):
    t, d = x.shape
    bsz, s, _ = a.shape
    tm = TOKEN_TILE
    tile = lambda w: pl.BlockSpec((1, tm, w), lambda b, j: (b, j, 0))
    branch_specs = []
    for _, dil in B_BRANCHES:
        branch_specs += [_class_view_spec(tm, dil)] * 2
    out = pl.pallas_call(
        _even_out_body,
        grid=(bsz, s // tm),
        in_specs=[tile(d), tile(A_WIDTH)] + branch_specs + [_const_spec(w_out.shape)],
        out_specs=tile(d),
        out_shape=jax.ShapeDtypeStruct((bsz, s, d), _F32),
        scratch_shapes=[pltpu.VMEM((2 * len(B_BRANCHES), B_WIDTH // LANES, tm, LANES), _F32)],
        compiler_params=_params("parallel", "parallel"),
        name="even_out",
    )(x.reshape(bsz, s, d), a, *[arr for pair in branches for arr in pair], w_out.astype(_BF16))
    return out.reshape(t, d)


def _rope(x, cos, sin_signed, low):
    partner = jnp.where(low, pltpu.roll(x, LANES - C_ROPE // 2, 1), pltpu.roll(x, C_ROPE // 2, 1))
    return x * cos + partner * sin_signed


def _odd_in_body(x_ref, g_ref, win_ref, gcq_ref, gckv_ref, wqn_ref, wqr_ref, wkn_ref, wv_ref,
                 gqn_ref, gqr_ref, gkn_ref, gkr_ref, cos_ref, sin_ref, q_ref, k_ref, v_ref):
    x = x_ref[0]
    h = _rms(x, g_ref[...]).astype(_BF16)
    z = _dot(h, win_ref[...])
    cq = _rms(z[:, :C_Q_LORA], gcq_ref[...]).astype(_BF16)
    ckv = _rms(z[:, C_Q_LORA:C_Q_LORA + C_KV_LORA], gckv_ref[...]).astype(_BF16)
    kr = z[:, C_Q_LORA + C_KV_LORA:]
    qn = _dot(cq, wqn_ref[...])
    qr = _dot(cq, wqr_ref[...])
    kn = _dot(ckv, wkn_ref[...])
    vv = _dot(ckv, wv_ref[...])
    cos, sin = cos_ref[...], sin_ref[...]
    lane = lax.broadcasted_iota(jnp.int32, (1, LANES), 1)
    low = (lane % C_ROPE) < (C_ROPE // 2)
    inv = 1.0 / (C_NOPE + C_ROPE)
    scale = (C_NOPE + C_ROPE) ** -0.5 * math.log2(math.e)
    kr_ss = jnp.sum(kr * kr, axis=-1, keepdims=True)
    kr_roped = _rope(kr * gkr_ref[...], cos, sin, low)
    for hd in range(C_HEADS):
        sl = slice(hd * LANES, (hd + 1) * LANES)
        qnh, qrh, knh = qn[:, sl], qr[:, sl], kn[:, sl]
        ssq = jnp.sum(qnh * qnh, axis=-1, keepdims=True) + jnp.sum(qrh * qrh, axis=-1, keepdims=True)
        rq = lax.rsqrt(ssq * inv + EPS) * scale
        q_ref[0, hd, :, :LANES] = (qnh * rq * gqn_ref[...]).astype(q_ref.dtype)
        q_ref[0, hd, :, LANES:] = (_rope(qrh * gqr_ref[...], cos, sin, low) * rq).astype(q_ref.dtype)
        ssk = jnp.sum(knh * knh, axis=-1, keepdims=True) + kr_ss
        rk = lax.rsqrt(ssk * inv + EPS)
        k_ref[0, hd, :, :LANES] = (knh * rk * gkn_ref[...]).astype(k_ref.dtype)
        k_ref[0, hd, :, LANES:] = (kr_roped * rk).astype(k_ref.dtype)
        v_ref[0, hd, :, :C_V] = vv[:, sl].astype(v_ref.dtype)
        v_ref[0, hd, :, C_V:] = jnp.ones((x.shape[0], C_V), v_ref.dtype)


def _pad_lanes(a, width):
    return jnp.pad(a, [(0, 0)] * (a.ndim - 1) + [(0, width - a.shape[-1])])


def _odd_in(x, g, w_in, g_cq, g_ckv, w_uq, w_ukv, g_q, g_k, bsz):
    t, d = x.shape
    s = t // bsz
    tm = TOKEN_TILE
    w_in_p = _pad_lanes(w_in, C_Q_LORA + C_KV_LORA + LANES).astype(_BF16)
    wq = w_uq.reshape(C_Q_LORA, C_HEADS, C_NOPE + C_ROPE)
    wqn = wq[:, :, :C_NOPE].reshape(C_Q_LORA, C_HEADS * C_NOPE).astype(_BF16)
    wqr = _pad_lanes(wq[:, :, C_NOPE:], LANES).reshape(C_Q_LORA, C_HEADS * LANES).astype(_BF16)
    wkv = w_ukv.reshape(C_KV_LORA, C_HEADS, C_NOPE + C_V)
    wkn = wkv[:, :, :C_NOPE].reshape(C_KV_LORA, C_HEADS * C_NOPE).astype(_BF16)
    wv = wkv[:, :, C_NOPE:].reshape(C_KV_LORA, C_HEADS * C_V).astype(_BF16)
    half = C_ROPE // 2
    inv_freq = ROPE_THETA ** (-jnp.arange(half, dtype=_F32) / half)
    ang = jnp.arange(s, dtype=_F32)[:, None] * inv_freq[None, :]
    cos = jnp.tile(jnp.cos(ang), (1, LANES // half))
    sin = jnp.tile(jnp.concatenate([-jnp.sin(ang), jnp.sin(ang)], axis=-1), (1, LANES // C_ROPE))
    row = lambda a: a.reshape(1, -1)
    qk_shape = jax.ShapeDtypeStruct((bsz, C_HEADS, s, C_QK_PAD), _BF16)
    head_spec = lambda w: pl.BlockSpec((1, C_HEADS, tm, w), lambda b, j: (b, 0, j, 0))
    pos_spec = pl.BlockSpec((tm, LANES), lambda b, j: (j, 0))
    consts = [row(g), w_in_p, row(g_cq), row(g_ckv), wqn, wqr, wkn, wv,
              row(g_q[:C_NOPE]), row(_pad_lanes(g_q[C_NOPE:], LANES)),
              row(g_k[:C_NOPE]), row(_pad_lanes(g_k[C_NOPE:], LANES))]
    return pl.pallas_call(
        _odd_in_body,
        grid=(bsz, s // tm),
        in_specs=[pl.BlockSpec((1, tm, d), lambda b, j: (b, j, 0))]
        + [_const_spec(c.shape) for c in consts] + [pos_spec, pos_spec],
        out_specs=[head_spec(C_QK_PAD), head_spec(C_QK_PAD), head_spec(2 * C_V)],
        out_shape=[qk_shape, qk_shape, jax.ShapeDtypeStruct((bsz, C_HEADS, s, 2 * C_V), _BF16)],
        compiler_params=_params("parallel", "parallel"),
        name="odd_in",
    )(x.reshape(bsz, s, d), *consts, cos, sin)


FLASH_ROWS = 128
FLASH_TQ = 1024


def _flash_body(q_ref, k_ref, v_ref, o_ref, s_ref, m_ref, acc_ref):
    qi = pl.program_id(2)
    tq = q_ref.shape[2]
    tk = tq
    dv = o_ref.shape[2]
    m_ref[...] = jnp.full_like(m_ref, NEG_INF)
    acc_ref[...] = jnp.zeros_like(acc_ref)

    def scores(j):
        start = pl.multiple_of(j * tk, tk)
        return _dot_nt(q_ref[0, 0], k_ref[0, 0, pl.ds(start, tk), :])

    def softmax_pv(s_all, j, diagonal):
        start = pl.multiple_of(j * tk, tk)
        v = v_ref[0, 0, pl.ds(start, tk), :]
        if diagonal:
            r = lax.broadcasted_iota(jnp.int32, s_all.shape, 0)
            col = lax.broadcasted_iota(jnp.int32, s_all.shape, 1)
            s_all = jnp.where(col <= r, s_all, NEG_INF)
        ps, scales = [], []
        for c in range(tq // FLASH_ROWS):
            rows = slice(c * FLASH_ROWS, (c + 1) * FLASH_ROWS)
            groups = [s_all[rows, g * LANES:(g + 1) * LANES] for g in range(tk // LANES)]
            m_old = m_ref[rows]
            gmax = functools.reduce(jnp.maximum, groups)
            m_new = jnp.maximum(m_old, jnp.max(gmax, axis=-1, keepdims=True))
            alpha = jnp.exp2(m_old - m_new)
            ps.append(jnp.concatenate([jnp.exp2((g - m_new).astype(_BF16)) for g in groups], axis=1))
            scales.append(jnp.concatenate([alpha] * (acc_ref.shape[1] // LANES), axis=1))
            m_ref[rows] = m_new
        pv = _dot(jnp.concatenate(ps, axis=0), v)
        acc_ref[...] = jnp.concatenate(scales, axis=0) * acc_ref[...] + pv

    s_ref[...] = scores(0)

    def full_tile(j, carry):
        s_cur = s_ref[...]
        s_next = scores(j + 1)
        softmax_pv(s_cur, j, False)
        s_ref[...] = s_next
        return carry

    lax.fori_loop(0, qi, full_tile, 0)
    softmax_pv(s_ref[...], qi, True)
    acc = acc_ref[...]
    o_ref[0] = (acc[:, :dv] / acc[:, dv:2 * dv]).astype(o_ref.dtype)


def _flash(q, k, v, tq):
    bsz, nh, s, dq = q.shape
    dv2 = v.shape[-1]
    dv = dv2 // 2
    whole = lambda w: pl.BlockSpec((1, 1, s, w), lambda b, h, i: (b, h, 0, 0))
    return pl.pallas_call(
        _flash_body,
        grid=(bsz, nh, s // tq),
        in_specs=[pl.BlockSpec((1, 1, tq, dq), lambda b, h, i: (b, h, i, 0)), whole(dq), whole(dv2)],
        out_specs=pl.BlockSpec((1, tq, dv), lambda b, h, i: (b, i, h)),
        out_shape=jax.ShapeDtypeStruct((bsz, s, nh * dv), _BF16),
        scratch_shapes=[pltpu.VMEM((tq, tq), _F32), pltpu.VMEM((tq, LANES), _F32), pltpu.VMEM((tq, dv2), _F32)],
        compiler_params=_params("parallel", "parallel", "arbitrary"),
        name="mla_flash",
    )(q, k, v)


def _proj_out_body(x_ref, o_ref, w_ref, out_ref):
    out_ref[...] = x_ref[...] + _dot(o_ref[...], w_ref[...])


def _proj_out(x, o, w_out):
    t, d = x.shape
    tm = TOKEN_TILE
    tile = lambda w: pl.BlockSpec((tm, w), lambda i: (i, 0))
    return pl.pallas_call(
        _proj_out_body,
        grid=(t // tm,),
        in_specs=[tile(d), tile(o.shape[1]), _const_spec(w_out.shape)],
        out_specs=tile(d),
        out_shape=jax.ShapeDtypeStruct((t, d), _F32),
        compiler_params=_params("parallel"),
        name="odd_out",
    )(x, o, w_out.astype(_BF16))


SUBLANES = 8


def _store_row_tiles(ref, val, lead=()):
    n = val.shape[0]
    for j in range(SUBLANES):
        ref[lead + (pl.ds(j, n, stride=SUBLANES), slice(None))] = val[:, j * LANES:(j + 1) * LANES]


def _load_row_tile_column(ref, n, j, lead=()):
    return ref[lead + (pl.ds(j, n, stride=SUBLANES), slice(None))]


def _row_tile(ref, row):
    return ref.at[pl.ds(pl.multiple_of(row * SUBLANES, SUBLANES), SUBLANES)]


def _router_body(x_ref, g_ref, wt_ref, b_ref, hp_ref, meta_ref, gate_ref, cnt_ref, base_ref):
    @pl.when(pl.program_id(0) == 0)
    def _():
        base_ref[...] = jnp.zeros_like(base_ref)

    h = _rms(x_ref[...], g_ref[...])
    tm = h.shape[0]
    _store_row_tiles(hp_ref, h)
    w = wt_ref[...]
    h_hi, w_hi = h.astype(_BF16), w.astype(_BF16)
    h_lo = (h - h_hi.astype(_F32)).astype(_BF16)
    w_lo = (w - w_hi.astype(_F32)).astype(_BF16)
    logits = _dot_nt(w_hi, h_hi) + _dot_nt(w_hi, h_lo) + _dot_nt(w_lo, h_hi) + b_ref[...]

    expert = lax.broadcasted_iota(jnp.int32, logits.shape, 0)
    hits, sels, tops = [], [], []
    work = logits
    for k in range(TOP_K):
        m = jnp.max(work, axis=0, keepdims=True)
        sel = jnp.min(jnp.where(work == m, expert, N_EXPERTS), axis=0, keepdims=True)
        hit = expert == sel
        work = jnp.where(hit, -jnp.inf, work)
        hits.append(hit)
        sels.append(sel)
        tops.append(m)
    r = lax.broadcasted_iota(jnp.int32, (tm, tm), 0)
    c = lax.broadcasted_iota(jnp.int32, (tm, tm), 1)
    earlier_tokens = (r < c).astype(_BF16)
    stacked = jnp.concatenate([hit.astype(_BF16) for hit in hits], axis=0)
    cum = _dot(stacked, earlier_tokens)
    base = base_ref[...][:, :1]
    ranks = []
    for k in range(TOP_K):
        onehot = hits[k].astype(_F32)
        before = cum[k * N_EXPERTS:(k + 1) * N_EXPERTS] + base
        ranks.append(jnp.sum(onehot * before, axis=0, keepdims=True).astype(jnp.int32))
        base = base + jnp.sum(onehot, axis=1, keepdims=True)
    base_ref[...] = jnp.broadcast_to(base, base_ref.shape)
    cnt_ref[...] = jnp.broadcast_to(base, cnt_ref.shape)
    exps = [jnp.ones_like(tops[0])] + [jnp.exp(tops[k] - tops[0]) for k in range(1, TOP_K)]
    denom = functools.reduce(lambda a_, b_: a_ + b_, exps)
    meta_ref[...] = jnp.concatenate(sels + ranks, axis=0)
    gate_ref[...] = jnp.concatenate([e / denom for e in exps] + [jnp.zeros_like(denom)] * TOP_K, axis=0)


def _router(x, g, w_r, b_r):
    t, d = x.shape
    tm = TOKEN_TILE
    rows = 2 * TOP_K
    lane_tile = pl.BlockSpec((rows, tm), lambda i: (0, i))
    hp, meta, gates, counts = pl.pallas_call(
        _router_body,
        grid=(t // tm,),
        in_specs=[pl.BlockSpec((tm, d), lambda i: (i, 0)), _const_spec((1, d)), _const_spec((N_EXPERTS, d)),
                  _const_spec((N_EXPERTS, 1))],
        out_specs=[pl.BlockSpec((tm * SUBLANES, LANES), lambda i: (i, 0)), lane_tile, lane_tile,
                   _const_spec((N_EXPERTS, LANES))],
        out_shape=[jax.ShapeDtypeStruct((t * SUBLANES, LANES), _F32), jax.ShapeDtypeStruct((rows, t), jnp.int32),
                   jax.ShapeDtypeStruct((rows, t), _F32), jax.ShapeDtypeStruct((N_EXPERTS, LANES), _F32)],
        scratch_shapes=[pltpu.VMEM((N_EXPERTS, LANES), _F32)],
        compiler_params=_params("arbitrary"),
        name="moe_router",
    )(x, g.reshape(1, d), w_r.T, b_r.reshape(N_EXPERTS, 1))
    return hp, meta, gates, counts[:, 0]


def _dispatch_body(slot_ref, hp_ref, xb_in_ref, xb_ref, sem):
    del xb_in_ref
    tm = hp_ref.shape[0] // SUBLANES

    def start(r, carry):
        for k in range(TOP_K):
            s = slot_ref[r * TOP_K + k]
            pltpu.make_async_copy(_row_tile(hp_ref, r), _row_tile(xb_ref, s), sem).start(priority=k % 2)
        return carry

    lax.fori_loop(0, tm, start, 0, unroll=2)
    for _ in range(TOP_K):
        pltpu.make_async_copy(hp_ref, xb_ref.at[pl.ds(0, tm * SUBLANES)], sem).wait()


def _dispatch(hp, slot_flat, xb_init):
    t = hp.shape[0] // SUBLANES
    tm = TOKEN_TILE
    return pl.pallas_call(
        _dispatch_body,
        grid=(t // tm,),
        in_specs=[pl.BlockSpec((tm * TOP_K,), lambda i: (i,), memory_space=pltpu.SMEM),
                  pl.BlockSpec((tm * SUBLANES, LANES), lambda i: (i, 0)),
                  pl.BlockSpec(memory_space=pl.ANY)],
        out_specs=pl.BlockSpec(memory_space=pl.ANY),
        out_shape=jax.ShapeDtypeStruct(xb_init.shape, hp.dtype),
        scratch_shapes=[pltpu.SemaphoreType.DMA(())],
        input_output_aliases={2: 0},
        compiler_params=_params("arbitrary"),
        name="moe_dispatch",
    )(slot_flat, hp, xb_init)


def _expert_body(be_ref, nu_ref, xb_ref, w1_ref, b1g_ref, b1l_ref, w2_ref, b2_ref, perm_ref,
                 yb_ref, w1g_s, w1l_s, w2_s):
    b = pl.program_id(0)
    live = b < nu_ref[0]
    changed = (b == 0) | (be_ref[b] != be_ref[jnp.maximum(b - 1, 0)])

    @pl.when(live & changed)
    def _():
        group = perm_ref.shape[0]
        for c in range(w1_ref.shape[2] // group):
            wc = w1_ref[0, :, c * group:(c + 1) * group].astype(_BF16)
            d = _dot(wc, perm_ref[...]).astype(_BF16)
            w1g_s[:, c * (group // 2):(c + 1) * (group // 2)] = d[:, :group // 2]
            w1l_s[:, c * (group // 2):(c + 1) * (group // 2)] = d[:, group // 2:]
        w2_s[...] = w2_ref[0].astype(_BF16)

    @pl.when(live)
    def _():
        x = jnp.concatenate([_load_row_tile_column(xb_ref, MOE_BLOCK, j) for j in range(SUBLANES)],
                            axis=1).astype(_BF16)
        glu = _dot(x, w1g_s[...]) + b1g_ref[0]
        lin = _dot(x, w1l_s[...]) + b1l_ref[0]
        glu = jnp.minimum(glu, SWIGLU_LIMIT)
        lin = jnp.clip(lin, -SWIGLU_LIMIT, SWIGLU_LIMIT)
        act = glu * jax.nn.sigmoid(SWIGLU_ALPHA * glu) * (lin + 1.0)
        _store_row_tiles(yb_ref, _dot(act.astype(_BF16), w2_s[...]) + b2_ref[0])

    @pl.when(jnp.logical_not(live))
    def _():
        yb_ref[...] = jnp.zeros_like(yb_ref)


def _experts(xb, block_expert, n_used, layer, w1, b1, w2, b2):
    n_rows = xb.shape[0] // SUBLANES
    n_blocks = n_rows // MOE_BLOCK
    _, n_e, d, two_f = w1.shape
    row_spec = lambda idx: pl.BlockSpec((MOE_BLOCK * SUBLANES, LANES), lambda b, be, nu: (idx(b, be, nu), 0))
    f = two_f // 2
    group = 2 * LANES
    j = jnp.arange(group)
    src = jnp.where(j < LANES, 2 * j, 2 * (j - LANES) + 1)
    perm = (jnp.arange(group)[:, None] == src[None, :]).astype(_BF16)
    b1p = b1.reshape(n_e, 1, f, 2)
    blk = lambda b, be, nu: jnp.minimum(b, nu[0] - 1)
    e_spec = lambda shape: pl.BlockSpec((1,) + shape, lambda b, be, nu: (be[b],) + (0,) * len(shape))
    w_spec = lambda shape: pl.BlockSpec((None, 1) + shape, lambda b, be, nu: (layer, be[b], 0, 0))
    grid_spec = pltpu.PrefetchScalarGridSpec(
        num_scalar_prefetch=2,
        grid=(n_blocks,),
        in_specs=[row_spec(blk),
                  w_spec((d, two_f)), e_spec((1, f)), e_spec((1, f)), w_spec((f, d)), e_spec((1, d)),
                  pl.BlockSpec((group, group), lambda b, be, nu: (0, 0))],
        out_specs=row_spec(lambda b, be, nu: b),
        scratch_shapes=[pltpu.VMEM((d, f), _BF16), pltpu.VMEM((d, f), _BF16), pltpu.VMEM((f, d), _BF16)],
    )
    return pl.pallas_call(
        _expert_body,
        grid_spec=grid_spec,
        out_shape=jax.ShapeDtypeStruct((n_rows * SUBLANES, LANES), _F32),
        compiler_params=_params("arbitrary"),
        name="moe_experts",
    )(block_expert, n_used, xb, w1, b1p[..., 0], b1p[..., 1], w2, b2.reshape(n_e, 1, d), perm)


def _combine_body(slot_ref, next_slot_ref, x_ref, gate_ref, yb_ref, out_ref, rows_ref, sems):
    i = pl.program_id(0)
    tm = x_ref.shape[0]

    def gather(slots, p):
        def start(r, carry):
            for k in range(TOP_K):
                s = slots[r * TOP_K + k]
                pltpu.make_async_copy(_row_tile(yb_ref, s), _row_tile(rows_ref.at[p, k], r),
                                      sems.at[p]).start(priority=k % 2)
            return carry

        lax.fori_loop(0, tm, start, 0, unroll=2)

    def consume(p):
        for k in range(TOP_K):
            pltpu.make_async_copy(yb_ref.at[pl.ds(0, tm * SUBLANES)], rows_ref.at[p, k], sems.at[p]).wait()
        out_ref[...] = x_ref[...]
        for k in range(TOP_K):
            gate = jnp.broadcast_to(gate_ref[:, k:k + 1], (tm, LANES))
            for j in range(SUBLANES):
                cols = slice(j * LANES, (j + 1) * LANES)
                out_ref[:, cols] += gate * _load_row_tile_column(rows_ref, tm, j, (p, k))

    @pl.when(i == 0)
    def _():
        gather(slot_ref, 0)

    for p in range(2):
        @pl.when(i % 2 == p)
        def _():
            @pl.when(i + 1 < pl.num_programs(0))
            def _():
                gather(next_slot_ref, 1 - p)

            consume(p)


def _combine(x, gates, slot_flat, yb):
    t, d = x.shape
    tm = TOKEN_TILE
    last = t // tm - 1
    return pl.pallas_call(
        _combine_body,
        grid=(t // tm,),
        in_specs=[pl.BlockSpec((tm * TOP_K,), lambda i: (i,), memory_space=pltpu.SMEM),
                  pl.BlockSpec((tm * TOP_K,), lambda i: (jnp.minimum(i + 1, last),), memory_space=pltpu.SMEM),
                  pl.BlockSpec((tm, d), lambda i: (i, 0)),
                  pl.BlockSpec((tm, LANES), lambda i: (i, 0)),
                  pl.BlockSpec(memory_space=pl.ANY)],
        out_specs=pl.BlockSpec((tm, d), lambda i: (i, 0)),
        out_shape=jax.ShapeDtypeStruct((t, d), _F32),
        scratch_shapes=[pltpu.VMEM((2, TOP_K, tm * SUBLANES, LANES), _F32), pltpu.SemaphoreType.DMA((2,))],
        compiler_params=_params("arbitrary"),
        name="moe_combine",
    )(slot_flat, slot_flat, x, gates, yb)


def _moe_blocks(t):
    return -(-(t * TOP_K + N_EXPERTS * (MOE_BLOCK - 1)) // MOE_BLOCK)


def _moe(x, g, w_r, b_r, layer, w1, b1, w2, b2, xb_init):
    t = x.shape[0]
    n_blocks = _moe_blocks(t)
    hp, meta, gates_t, counts = _router(x, g, w_r, b_r)
    counts = counts.astype(jnp.int32)
    padded = (counts + MOE_BLOCK - 1) // MOE_BLOCK * MOE_BLOCK
    pad_end = jnp.cumsum(padded)
    pad_start = pad_end - padded
    experts = jnp.arange(N_EXPERTS, dtype=jnp.int32)
    start_of = jnp.sum(jnp.where(meta[:TOP_K, :, None] == experts, pad_start, 0), axis=-1)
    slot_flat = (start_of + meta[TOP_K:]).T.reshape(-1).astype(jnp.int32)
    gates = _pad_lanes(gates_t[:TOP_K].T, LANES)
    n_used = (pad_end[-1] // MOE_BLOCK).astype(jnp.int32)
    block_id = jnp.minimum(jnp.arange(n_blocks, dtype=jnp.int32), n_used - 1)
    block_expert = jnp.sum(pad_end[None, :] <= (block_id * MOE_BLOCK)[:, None], axis=-1).astype(jnp.int32)
    xb = _dispatch(hp, slot_flat, xb_init)
    yb = _experts(xb, block_expert, n_used.reshape(1), layer, w1, b1, w2, b2)
    return _combine(x, gates, slot_flat, yb), xb


def kernel(x, mix_norm, ffn_norm, even_w_in, even_w_s, even_b_s, even_g_v, even_g_q, even_g_k, even_w_out, odd_w_in, odd_g_cq, odd_g_ckv, odd_w_uq, odd_w_ukv, odd_g_q, odd_g_k, odd_w_out, router_w, router_b, expert_w1, expert_b1, expert_w2, expert_b2):
    bsz, s, d = x.shape
    xt = x.reshape(bsz * s, d)
    xb = jnp.zeros((_moe_blocks(bsz * s) * MOE_BLOCK * SUBLANES, LANES), _F32)
    for l in range(mix_norm.shape[0]):
        i = l // 2
        if l % 2 == 0:
            a, qkv = _even_in(xt, mix_norm[l], even_w_in[i], even_w_s[i], even_b_s[i], even_g_v[i],
                              even_g_q[i], even_g_k[i], bsz)
            xt = _even_out(xt, a, _dilated_attention(qkv), even_w_out[i])
        else:
            q, k, v = _odd_in(xt, mix_norm[l], odd_w_in[i], odd_g_cq[i], odd_g_ckv[i], odd_w_uq[i],
                              odd_w_ukv[i], odd_g_q[i], odd_g_k[i], bsz)
            o = _flash(q, k, v, FLASH_TQ)
            xt = _proj_out(xt, o.reshape(bsz * s, -1), odd_w_out[i])
        xt, xb = _moe(xt, ffn_norm[l], router_w[l], router_b[l], l, expert_w1, expert_b1[l],
                      expert_w2, expert_b2[l], xb)
    return xt.reshape(bsz, s, d)
```

```python
import functools
import math

import jax
import jax.numpy as jnp
from jax import lax
from jax.experimental import pallas as pl
from jax.experimental.pallas import tpu as pltpu

EPS = 1e-6
NEG_INF = -1e30
A_HEADS = 4
A_HEAD_DIM = 128
A_CHUNK = 128
A_WIDTH = A_HEADS * A_HEAD_DIM
B_HEADS = 8
B_HEAD_DIM = 64
B_WIDTH = B_HEADS * B_HEAD_DIM
B_BRANCHES = ((128, 1), (512, 4), (2048, 16))
B_BLOCK = 128
C_HEADS = 8
C_NOPE = 128
C_ROPE = 64
C_V = 128
C_Q_LORA = 512
C_KV_LORA = 256
C_QK_PAD = 256
ROPE_THETA = 10000.0
N_EXPERTS = 32
TOP_K = 4
D_EXPERT = 1024
SWIGLU_ALPHA = 1.702
SWIGLU_LIMIT = 7.0
MOE_BLOCK = 512

LANES = 128
TOKEN_TILE = 256
ROW_MOVE_TILE = 512
VMEM_LIMIT = 56 * 1024 * 1024

_F32 = jnp.float32
_BF16 = jnp.bfloat16


def _params(*sem):
    return pltpu.CompilerParams(dimension_semantics=sem, vmem_limit_bytes=VMEM_LIMIT)


def _rms(x, g):
    return x * lax.rsqrt(jnp.mean(x * x, axis=-1, keepdims=True) + EPS) * g


def _gelu(x):
    return 0.5 * x * (1.0 + lax.erf(x * math.sqrt(0.5)))


def _dot(a, b):
    return jnp.dot(a, b, preferred_element_type=_F32)


def _dot_nt(a, b):
    return lax.dot_general(a, b, (((1,), (1,)), ((), ())), preferred_element_type=_F32)


def _seg_sum(x2, seg):
    hi = x2.astype(_BF16)
    lo = (x2 - hi.astype(_F32)).astype(_BF16)
    return _dot(hi, seg) + _dot(lo, seg)


def _const_spec(shape):
    nd = len(shape)
    return pl.BlockSpec(shape, lambda *_: (0,) * nd)


def _even_in_body(x_ref, g_ref, win_ref, ws_ref, bst_ref, gv_ref, gq_ref, gk_ref, seg_ref,
                  a_ref, *rest):
    qkv_refs, stage_ref = rest[:-1], rest[-1]
    x = x_ref[0]
    h = _rms(x, g_ref[...]).astype(_BF16)
    z = _dot(h, win_ref[...])
    tm = x.shape[0]
    u = _gelu(z[:, :A_WIDTH])
    vv = _gelu(z[:, A_WIDTH:2 * A_WIDTH])
    row = lax.broadcasted_iota(jnp.int32, (A_CHUNK, A_CHUNK), 0)
    col = lax.broadcasted_iota(jnp.int32, (A_CHUNK, A_CHUNK), 1)
    for hd in range(A_HEADS):
        sl = slice(hd * A_HEAD_DIM, (hd + 1) * A_HEAD_DIM)
        vh = _rms(vv[:, sl], gv_ref[hd:hd + 1, :]).astype(_BF16)
        w = jnp.where(row >= col, ws_ref[hd], 0.0).astype(_BF16)
        bias = bst_ref[:, hd:hd + 1]
        for c in range(tm // A_CHUNK):
            rs = slice(c * A_CHUNK, (c + 1) * A_CHUNK)
            mixed = _dot(w, vh[rs]) + bias
            a_ref[0, rs, sl] = (u[rs, sl] * mixed).astype(a_ref.dtype)
    o = 2 * A_WIDTH
    q = z[:, o:o + B_WIDTH]
    k = z[:, o + B_WIDTH:o + 2 * B_WIDTH]
    seg = seg_ref[...]
    inv = 1.0 / B_HEAD_DIM
    qn = q * lax.rsqrt(_seg_sum(q * q, seg) * inv + EPS) * gq_ref[...]
    kn = k * lax.rsqrt(_seg_sum(k * k, seg) * inv + EPS) * gk_ref[...]
    qkv = (qn * (B_HEAD_DIM ** -0.5), kn, z[:, o + 2 * B_WIDTH:])
    groups = B_WIDTH // LANES
    for which, val in enumerate(qkv):
        for gi in range(groups):
            stage_ref[gi] = val[:, gi * LANES:(gi + 1) * LANES]
        for bi, (_, dil) in enumerate(B_BRANCHES):
            ref = qkv_refs[3 * bi + which]
            if dil == 1:
                ref[0] = val.astype(ref.dtype)
                continue
            for r in range(dil):
                for gi in range(groups):
                    rows = stage_ref[gi, pl.ds(r, tm // dil, stride=dil), :]
                    lo = r * B_WIDTH + gi * LANES
                    ref[0, :, lo:lo + LANES] = rows.astype(ref.dtype)


def _class_view_spec(tm, dil):
    return pl.BlockSpec((1, tm // dil, dil * B_WIDTH), lambda b, j: (b, j, 0))


def _even_in(x, g, w_in, w_s, b_s, g_v, g_q, g_k, bsz):
    t, d = x.shape
    s = t // bsz
    tm = TOKEN_TILE
    seg = (jnp.arange(B_WIDTH)[:, None] // B_HEAD_DIM == jnp.arange(B_WIDTH)[None, :] // B_HEAD_DIM).astype(_BF16)
    tile = lambda w: pl.BlockSpec((1, tm, w), lambda b, j: (b, j, 0))
    qkv_specs, qkv_shapes = [], []
    for _, dil in B_BRANCHES:
        qkv_specs += [_class_view_spec(tm, dil)] * 3
        qkv_shapes += [jax.ShapeDtypeStruct((bsz, s // dil, dil * B_WIDTH), _BF16)] * 3
    outs = pl.pallas_call(
        _even_in_body,
        grid=(bsz, s // tm),
        in_specs=[tile(d), _const_spec((1, d)), _const_spec(w_in.shape), _const_spec(w_s.shape),
                  _const_spec((A_CHUNK, A_HEADS)), _const_spec(g_v.shape), _const_spec((1, B_WIDTH)),
                  _const_spec((1, B_WIDTH)), _const_spec(seg.shape)],
        out_specs=[tile(A_WIDTH)] + qkv_specs,
        out_shape=[jax.ShapeDtypeStruct((bsz, s, A_WIDTH), _BF16)] + qkv_shapes,
        scratch_shapes=[pltpu.VMEM((B_WIDTH // LANES, tm, LANES), _F32)],
        compiler_params=_params("parallel", "parallel"),
        name="even_in",
    )(x.reshape(bsz, s, d), g.reshape(1, d), w_in.astype(_BF16), w_s, b_s.T, g_v,
      jnp.tile(g_q, B_HEADS).reshape(1, B_WIDTH), jnp.tile(g_k, B_HEADS).reshape(1, B_WIDTH), seg)
    return outs[0], [outs[1 + 3 * bi:4 + 3 * bi] for bi in range(len(B_BRANCHES))]


def _dilated_body(*refs, blocks_per_class):
    nb = len(blocks_per_class)
    t = pl.program_id(1)
    for bi, n in enumerate(blocks_per_class):
        _dilated_block(*refs[5 * bi:5 * bi + 5], *refs[5 * nb + 2 * bi:5 * nb + 2 * bi + 2], t % n)


def _dilated_block(q_ref, kp_ref, kc_ref, vp_ref, vc_ref, o_ref, lse_ref, jb):
    q = q_ref[0]
    kk = jnp.concatenate([kp_ref[0], kc_ref[0]], axis=0)
    vv = jnp.concatenate([vp_ref[0], vc_ref[0]], axis=0)
    i = lax.broadcasted_iota(jnp.int32, (B_BLOCK, 2 * B_BLOCK), 0)
    c = lax.broadcasted_iota(jnp.int32, (B_BLOCK, 2 * B_BLOCK), 1)
    valid = (c >= i) & (c <= i + B_BLOCK) & ((jb > 0) | (c >= B_BLOCK))
    lane = lax.broadcasted_iota(jnp.int32, (1, LANES), 1)
    first = lane < B_HEAD_DIM
    for p in range(B_WIDTH // LANES):
        sl = slice(p * LANES, (p + 1) * LANES)
        q2, k2, v2 = q[:, sl], kk[:, sl], vv[:, sl]
        outs, lses = [], []
        for half in (first, jnp.logical_not(first)):
            s = _dot_nt(jnp.where(half, q2, jnp.zeros_like(q2)), k2)
            s = jnp.where(valid, s, NEG_INF)
            m = jnp.max(s, axis=-1, keepdims=True)
            e = jnp.exp(s - m)
            l = jnp.sum(e, axis=-1, keepdims=True)
            outs.append(_dot(e.astype(_BF16), v2) / l)
            lses.append(m + jnp.log(l))
        o_ref[0, :, sl] = jnp.where(first, outs[0], outs[1])
        lse_ref[0, :, sl] = jnp.where(first, lses[0], lses[1])


def _dilated_attention(qkv):
    bsz, s, _ = qkv[0][0].shape
    steps = s // B_BLOCK
    in_specs, out_specs, out_shapes, args, blocks_per_class = [], [], [], [], []
    for (q, k, v), (_, dil) in zip(qkv, B_BRANCHES):
        n = steps // dil
        cur = pl.BlockSpec((1, B_BLOCK, B_WIDTH), lambda b, t, n=n: (b, t % n, t // n))
        prev = pl.BlockSpec((1, B_BLOCK, B_WIDTH), lambda b, t, n=n: (b, jnp.maximum(t % n - 1, 0), t // n))
        in_specs += [cur, prev, cur, prev, cur]
        out_specs += [cur, cur]
        out_shapes += [jax.ShapeDtypeStruct(q.shape, _F32)] * 2
        args += [q, k, k, v, v]
        blocks_per_class.append(n)
    outs = pl.pallas_call(
        functools.partial(_dilated_body, blocks_per_class=tuple(blocks_per_class)),
        grid=(bsz, steps),
        in_specs=in_specs,
        out_specs=out_specs,
        out_shape=out_shapes,
        compiler_params=_params("parallel", "parallel"),
        name="dilated_attention",
    )(*args)
    return [(outs[2 * bi], outs[2 * bi + 1]) for bi in range(len(B_BRANCHES))]


def _even_out_body(x_ref, a_ref, *rest):
    nb = len(B_BRANCHES)
    branch_refs, w_ref, out_ref, stage_ref = rest[:2 * nb], rest[2 * nb], rest[2 * nb + 1], rest[2 * nb + 2]
    tm = x_ref.shape[1]

    def token_order(ref, dil, slot):
        if dil == 1:
            return ref[0]
        groups = B_WIDTH // LANES
        for r in range(dil):
            for gi in range(groups):
                lo = r * B_WIDTH + gi * LANES
                stage_ref[slot, gi, pl.ds(r, tm // dil, stride=dil), :] = ref[0, :, lo:lo + LANES]
        return jnp.concatenate([stage_ref[slot, gi] for gi in range(groups)], axis=1)

    outs = [token_order(branch_refs[2 * bi], dil, 2 * bi) for bi, (_, dil) in enumerate(B_BRANCHES)]
    lses = [token_order(branch_refs[2 * bi + 1], dil, 2 * bi + 1) for bi, (_, dil) in enumerate(B_BRANCHES)]
    m = functools.reduce(jnp.maximum, lses)
    es = [jnp.exp(l - m) for l in lses]
    b = sum(e * o for e, o in zip(es, outs)) / sum(es)
    y = _dot(a_ref[0], w_ref[:A_WIDTH, :]) + _dot(b.astype(_BF16), w_ref[A_WIDTH:, :])
    out_ref[0] = x_ref[0] + y


def _even_out(x, a, branches, w_out):
    t, d = x.shape
    bsz, s, _ = a.shape
    tm = TOKEN_TILE
    tile = lambda w: pl.BlockSpec((1, tm, w), lambda b, j: (b, j, 0))
    branch_specs = []
    for _, dil in B_BRANCHES:
        branch_specs += [_class_view_spec(tm, dil)] * 2
    out = pl.pallas_call(
        _even_out_body,
        grid=(bsz, s // tm),
        in_specs=[tile(d), tile(A_WIDTH)] + branch_specs + [_const_spec(w_out.shape)],
        out_specs=tile(d),
        out_shape=jax.ShapeDtypeStruct((bsz, s, d), _F32),
        scratch_shapes=[pltpu.VMEM((2 * len(B_BRANCHES), B_WIDTH // LANES, tm, LANES), _F32)],
        compiler_params=_params("parallel", "parallel"),
        name="even_out",
    )(x.reshape(bsz, s, d), a, *[arr for pair in branches for arr in pair], w_out.astype(_BF16))
    return out.reshape(t, d)


def _rope(x, cos, sin_signed, low):
    partner = jnp.where(low, pltpu.roll(x, LANES - C_ROPE // 2, 1), pltpu.roll(x, C_ROPE // 2, 1))
    return x * cos + partner * sin_signed


def _odd_in_body(x_ref, g_ref, win_ref, gcq_ref, gckv_ref, wqn_ref, wqr_ref, wkn_ref, wv_ref,
                 gqn_ref, gqr_ref, gkn_ref, gkr_ref, cos_ref, sin_ref, q_ref, k_ref, v_ref):
    x = x_ref[0]
    h = _rms(x, g_ref[...]).astype(_BF16)
    z = _dot(h, win_ref[...])
    cq = _rms(z[:, :C_Q_LORA], gcq_ref[...]).astype(_BF16)
    ckv = _rms(z[:, C_Q_LORA:C_Q_LORA + C_KV_LORA], gckv_ref[...]).astype(_BF16)
    kr = z[:, C_Q_LORA + C_KV_LORA:]
    qn = _dot(cq, wqn_ref[...])
    qr = _dot(cq, wqr_ref[...])
    kn = _dot(ckv, wkn_ref[...])
    vv = _dot(ckv, wv_ref[...])
    cos, sin = cos_ref[...], sin_ref[...]
    lane = lax.broadcasted_iota(jnp.int32, (1, LANES), 1)
    low = (lane % C_ROPE) < (C_ROPE // 2)
    inv = 1.0 / (C_NOPE + C_ROPE)
    scale = (C_NOPE + C_ROPE) ** -0.5 * math.log2(math.e)
    kr_ss = jnp.sum(kr * kr, axis=-1, keepdims=True)
    kr_roped = _rope(kr * gkr_ref[...], cos, sin, low)
    for hd in range(C_HEADS):
        sl = slice(hd * LANES, (hd + 1) * LANES)
        qnh, qrh, knh = qn[:, sl], qr[:, sl], kn[:, sl]
        ssq = jnp.sum(qnh * qnh, axis=-1, keepdims=True) + jnp.sum(qrh * qrh, axis=-1, keepdims=True)
        rq = lax.rsqrt(ssq * inv + EPS) * scale
        q_ref[0, hd, :, :LANES] = (qnh * rq * gqn_ref[...]).astype(q_ref.dtype)
        q_ref[0, hd, :, LANES:] = (_rope(qrh * gqr_ref[...], cos, sin, low) * rq).astype(q_ref.dtype)
        ssk = jnp.sum(knh * knh, axis=-1, keepdims=True) + kr_ss
        rk = lax.rsqrt(ssk * inv + EPS)
        k_ref[0, hd, :, :LANES] = (knh * rk * gkn_ref[...]).astype(k_ref.dtype)
        k_ref[0, hd, :, LANES:] = (kr_roped * rk).astype(k_ref.dtype)
        v_ref[0, hd, :, :C_V] = vv[:, sl].astype(v_ref.dtype)
        v_ref[0, hd, :, C_V:] = jnp.ones((x.shape[0], C_V), v_ref.dtype)


def _pad_lanes(a, width):
    return jnp.pad(a, [(0, 0)] * (a.ndim - 1) + [(0, width - a.shape[-1])])


def _odd_in(x, g, w_in, g_cq, g_ckv, w_uq, w_ukv, g_q, g_k, bsz):
    t, d = x.shape
    s = t // bsz
    tm = TOKEN_TILE
    w_in_p = _pad_lanes(w_in, C_Q_LORA + C_KV_LORA + LANES).astype(_BF16)
    wq = w_uq.reshape(C_Q_LORA, C_HEADS, C_NOPE + C_ROPE)
    wqn = wq[:, :, :C_NOPE].reshape(C_Q_LORA, C_HEADS * C_NOPE).astype(_BF16)
    wqr = _pad_lanes(wq[:, :, C_NOPE:], LANES).reshape(C_Q_LORA, C_HEADS * LANES).astype(_BF16)
    wkv = w_ukv.reshape(C_KV_LORA, C_HEADS, C_NOPE + C_V)
    wkn = wkv[:, :, :C_NOPE].reshape(C_KV_LORA, C_HEADS * C_NOPE).astype(_BF16)
    wv = wkv[:, :, C_NOPE:].reshape(C_KV_LORA, C_HEADS * C_V).astype(_BF16)
    half = C_ROPE // 2
    inv_freq = ROPE_THETA ** (-jnp.arange(half, dtype=_F32) / half)
    ang = jnp.arange(s, dtype=_F32)[:, None] * inv_freq[None, :]
    cos = jnp.tile(jnp.cos(ang), (1, LANES // half))
    sin = jnp.tile(jnp.concatenate([-jnp.sin(ang), jnp.sin(ang)], axis=-1), (1, LANES // C_ROPE))
    row = lambda a: a.reshape(1, -1)
    qk_shape = jax.ShapeDtypeStruct((bsz, C_HEADS, s, C_QK_PAD), _BF16)
    head_spec = lambda w: pl.BlockSpec((1, C_HEADS, tm, w), lambda b, j: (b, 0, j, 0))
    pos_spec = pl.BlockSpec((tm, LANES), lambda b, j: (j, 0))
    consts = [row(g), w_in_p, row(g_cq), row(g_ckv), wqn, wqr, wkn, wv,
              row(g_q[:C_NOPE]), row(_pad_lanes(g_q[C_NOPE:], LANES)),
              row(g_k[:C_NOPE]), row(_pad_lanes(g_k[C_NOPE:], LANES))]
    return pl.pallas_call(
        _odd_in_body,
        grid=(bsz, s // tm),
        in_specs=[pl.BlockSpec((1, tm, d), lambda b, j: (b, j, 0))]
        + [_const_spec(c.shape) for c in consts] + [pos_spec, pos_spec],
        out_specs=[head_spec(C_QK_PAD), head_spec(C_QK_PAD), head_spec(2 * C_V)],
        out_shape=[qk_shape, qk_shape, jax.ShapeDtypeStruct((bsz, C_HEADS, s, 2 * C_V), _BF16)],
        compiler_params=_params("parallel", "parallel"),
        name="odd_in",
    )(x.reshape(bsz, s, d), *consts, cos, sin)


FLASH_ROWS = 128
FLASH_TQ = 1024


def _flash_body(q_ref, k_ref, v_ref, o_ref, s_ref, m_ref, acc_ref):
    qi = pl.program_id(2)
    tq = q_ref.shape[2]
    tk = tq
    dv = o_ref.shape[2]
    m_ref[...] = jnp.full_like(m_ref, NEG_INF)
    acc_ref[...] = jnp.zeros_like(acc_ref)

    def scores(j):
        start = pl.multiple_of(j * tk, tk)
        return _dot_nt(q_ref[0, 0], k_ref[0, 0, pl.ds(start, tk), :])

    def softmax_pv(s_all, j, diagonal):
        start = pl.multiple_of(j * tk, tk)
        v = v_ref[0, 0, pl.ds(start, tk), :]
        if diagonal:
            r = lax.broadcasted_iota(jnp.int32, s_all.shape, 0)
            col = lax.broadcasted_iota(jnp.int32, s_all.shape, 1)
            s_all = jnp.where(col <= r, s_all, NEG_INF)
        ps, scales = [], []
        for c in range(tq // FLASH_ROWS):
            rows = slice(c * FLASH_ROWS, (c + 1) * FLASH_ROWS)
            groups = [s_all[rows, g * LANES:(g + 1) * LANES] for g in range(tk // LANES)]
            m_old = m_ref[rows]
            gmax = functools.reduce(jnp.maximum, groups)
            m_new = jnp.maximum(m_old, jnp.max(gmax, axis=-1, keepdims=True))
            alpha = jnp.exp2(m_old - m_new)
            ps.append(jnp.concatenate([jnp.exp2((g - m_new).astype(_BF16)) for g in groups], axis=1))
            scales.append(jnp.concatenate([alpha] * (acc_ref.shape[1] // LANES), axis=1))
            m_ref[rows] = m_new
        pv = _dot(jnp.concatenate(ps, axis=0), v)
        acc_ref[...] = jnp.concatenate(scales, axis=0) * acc_ref[...] + pv

    s_ref[...] = scores(0)

    def full_tile(j, carry):
        s_cur = s_ref[...]
        s_next = scores(j + 1)
        softmax_pv(s_cur, j, False)
        s_ref[...] = s_next
        return carry

    lax.fori_loop(0, qi, full_tile, 0)
    softmax_pv(s_ref[...], qi, True)
    acc = acc_ref[...]
    o_ref[0] = (acc[:, :dv] / acc[:, dv:2 * dv]).astype(o_ref.dtype)


def _flash(q, k, v, tq):
    bsz, nh, s, dq = q.shape
    dv2 = v.shape[-1]
    dv = dv2 // 2
    whole = lambda w: pl.BlockSpec((1, 1, s, w), lambda b, h, i: (b, h, 0, 0))
    return pl.pallas_call(
        _flash_body,
        grid=(bsz, nh, s // tq),
        in_specs=[pl.BlockSpec((1, 1, tq, dq), lambda b, h, i: (b, h, i, 0)), whole(dq), whole(dv2)],
        out_specs=pl.BlockSpec((1, tq, dv), lambda b, h, i: (b, i, h)),
        out_shape=jax.ShapeDtypeStruct((bsz, s, nh * dv), _BF16),
        scratch_shapes=[pltpu.VMEM((tq, tq), _F32), pltpu.VMEM((tq, LANES), _F32), pltpu.VMEM((tq, dv2), _F32)],
        compiler_params=_params("parallel", "parallel", "arbitrary"),
        name="mla_flash",
    )(q, k, v)


def _proj_out_body(x_ref, o_ref, w_ref, out_ref):
    out_ref[...] = x_ref[...] + _dot(o_ref[...], w_ref[...])


def _proj_out(x, o, w_out):
    t, d = x.shape
    tm = TOKEN_TILE
    tile = lambda w: pl.BlockSpec((tm, w), lambda i: (i, 0))
    return pl.pallas_call(
        _proj_out_body,
        grid=(t // tm,),
        in_specs=[tile(d), tile(o.shape[1]), _const_spec(w_out.shape)],
        out_specs=tile(d),
        out_shape=jax.ShapeDtypeStruct((t, d), _F32),
        compiler_params=_params("parallel"),
        name="odd_out",
    )(x, o, w_out.astype(_BF16))


SUBLANES = 8


def _store_row_tiles(ref, val, lead=()):
    n = val.shape[0]
    for j in range(SUBLANES):
        ref[lead + (pl.ds(j, n, stride=SUBLANES), slice(None))] = val[:, j * LANES:(j + 1) * LANES]


def _load_row_tile_column(ref, n, j, lead=()):
    return ref[lead + (pl.ds(j, n, stride=SUBLANES), slice(None))]


def _row_tile(ref, row):
    return ref.at[pl.ds(pl.multiple_of(row * SUBLANES, SUBLANES), SUBLANES)]


def _router_body(x_ref, g_ref, wt_ref, b_ref, hp_ref, meta_ref, gate_ref, cnt_ref, base_ref):
    @pl.when(pl.program_id(0) == 0)
    def _():
        base_ref[...] = jnp.zeros_like(base_ref)

    h = _rms(x_ref[...], g_ref[...])
    tm = h.shape[0]
    _store_row_tiles(hp_ref, h)
    w = wt_ref[...]
    h_hi, w_hi = h.astype(_BF16), w.astype(_BF16)
    h_lo = (h - h_hi.astype(_F32)).astype(_BF16)
    w_lo = (w - w_hi.astype(_F32)).astype(_BF16)
    logits = _dot_nt(w_hi, h_hi) + _dot_nt(w_hi, h_lo) + _dot_nt(w_lo, h_hi) + b_ref[...]

    expert = lax.broadcasted_iota(jnp.int32, logits.shape, 0)
    hits, sels, tops = [], [], []
    work = logits
    for k in range(TOP_K):
        m = jnp.max(work, axis=0, keepdims=True)
        sel = jnp.min(jnp.where(work == m, expert, N_EXPERTS), axis=0, keepdims=True)
        hit = expert == sel
        work = jnp.where(hit, -jnp.inf, work)
        hits.append(hit)
        sels.append(sel)
        tops.append(m)
    r = lax.broadcasted_iota(jnp.int32, (tm, tm), 0)
    c = lax.broadcasted_iota(jnp.int32, (tm, tm), 1)
    earlier_tokens = (r < c).astype(_BF16)
    stacked = jnp.concatenate([hit.astype(_BF16) for hit in hits], axis=0)
    cum = _dot(stacked, earlier_tokens)
    base = base_ref[...][:, :1]
    ranks = []
    for k in range(TOP_K):
        onehot = hits[k].astype(_F32)
        before = cum[k * N_EXPERTS:(k + 1) * N_EXPERTS] + base
        ranks.append(jnp.sum(onehot * before, axis=0, keepdims=True).astype(jnp.int32))
        base = base + jnp.sum(onehot, axis=1, keepdims=True)
    base_ref[...] = jnp.broadcast_to(base, base_ref.shape)
    cnt_ref[...] = jnp.broadcast_to(base, cnt_ref.shape)
    exps = [jnp.ones_like(tops[0])] + [jnp.exp(tops[k] - tops[0]) for k in range(1, TOP_K)]
    denom = functools.reduce(lambda a_, b_: a_ + b_, exps)
    meta_ref[...] = jnp.concatenate(sels + ranks, axis=0)
    gate_ref[...] = jnp.concatenate([e / denom for e in exps] + [jnp.zeros_like(denom)] * TOP_K, axis=0)


def _router(x, g, w_r, b_r):
    t, d = x.shape
    tm = TOKEN_TILE
    rows = 2 * TOP_K
    lane_tile = pl.BlockSpec((rows, tm), lambda i: (0, i))
    hp, meta, gates, counts = pl.pallas_call(
        _router_body,
        grid=(t // tm,),
        in_specs=[pl.BlockSpec((tm, d), lambda i: (i, 0)), _const_spec((1, d)), _const_spec((N_EXPERTS, d)),
                  _const_spec((N_EXPERTS, 1))],
        out_specs=[pl.BlockSpec((tm * SUBLANES, LANES), lambda i: (i, 0)), lane_tile, lane_tile,
                   _const_spec((N_EXPERTS, LANES))],
        out_shape=[jax.ShapeDtypeStruct((t * SUBLANES, LANES), _F32), jax.ShapeDtypeStruct((rows, t), jnp.int32),
                   jax.ShapeDtypeStruct((rows, t), _F32), jax.ShapeDtypeStruct((N_EXPERTS, LANES), _F32)],
        scratch_shapes=[pltpu.VMEM((N_EXPERTS, LANES), _F32)],
        compiler_params=_params("arbitrary"),
        name="moe_router",
    )(x, g.reshape(1, d), w_r.T, b_r.reshape(N_EXPERTS, 1))
    return hp, meta, gates, counts[:, 0]


def _dispatch_body(slot_ref, hp_ref, xb_in_ref, xb_ref, sem):
    del xb_in_ref
    tm = hp_ref.shape[0] // SUBLANES

    def start(r, carry):
        for k in range(TOP_K):
            s = slot_ref[r * TOP_K + k]
            pltpu.make_async_copy(_row_tile(hp_ref, r), _row_tile(xb_ref, s), sem).start(priority=k % 2)
        return carry

    lax.fori_loop(0, tm, start, 0, unroll=2)
    for _ in range(TOP_K):
        pltpu.make_async_copy(hp_ref, xb_ref.at[pl.ds(0, tm * SUBLANES)], sem).wait()


def _dispatch(hp, slot_flat, xb_init):
    t = hp.shape[0] // SUBLANES
    tm = ROW_MOVE_TILE
    return pl.pallas_call(
        _dispatch_body,
        grid=(t // tm,),
        in_specs=[pl.BlockSpec((tm * TOP_K,), lambda i: (i,), memory_space=pltpu.SMEM),
                  pl.BlockSpec((tm * SUBLANES, LANES), lambda i: (i, 0)),
                  pl.BlockSpec(memory_space=pl.ANY)],
        out_specs=pl.BlockSpec(memory_space=pl.ANY),
        out_shape=jax.ShapeDtypeStruct(xb_init.shape, hp.dtype),
        scratch_shapes=[pltpu.SemaphoreType.DMA(())],
        input_output_aliases={2: 0},
        compiler_params=_params("arbitrary"),
        name="moe_dispatch",
    )(slot_flat, hp, xb_init)


def _expert_body(be_ref, nu_ref, nx_ref, par_ref, xb_ref, w1_hbm, b1g_ref, b1l_ref, w2_hbm, b2_ref, perm_ref,
                 yb_ref, w1_buf, w2_buf, w1g_s, w1l_s, w2_s, sems, *, layer):
    b = pl.program_id(0)
    live = b < nu_ref[0]
    changed = (b == 0) | (be_ref[b] != be_ref[jnp.maximum(b - 1, 0)])

    def fetch(e, slot):
        return (pltpu.make_async_copy(w1_hbm.at[layer, e], w1_buf.at[slot], sems.at[slot, 0]),
                pltpu.make_async_copy(w2_hbm.at[layer, e], w2_buf.at[slot], sems.at[slot, 1]))

    @pl.when(b == 0)
    def _():
        for copy in fetch(be_ref[0], 0):
            copy.start()

    @pl.when(live & changed)
    def _():
        slot = par_ref[b]
        for copy in fetch(be_ref[b], slot):
            copy.wait()

        @pl.when(nx_ref[b] >= 0)
        def _():
            for copy in fetch(nx_ref[b], 1 - slot):
                copy.start()

        group = perm_ref.shape[0]
        for c in range(w1_buf.shape[2] // group):
            wc = w1_buf[slot, :, c * group:(c + 1) * group].astype(_BF16)
            d = _dot(wc, perm_ref[...]).astype(_BF16)
            w1g_s[:, c * (group // 2):(c + 1) * (group // 2)] = d[:, :group // 2]
            w1l_s[:, c * (group // 2):(c + 1) * (group // 2)] = d[:, group // 2:]
        w2_s[...] = w2_buf[slot].astype(_BF16)

    @pl.when(live)
    def _():
        x = jnp.concatenate([_load_row_tile_column(xb_ref, MOE_BLOCK, j) for j in range(SUBLANES)],
                            axis=1).astype(_BF16)
        glu = _dot(x, w1g_s[...]) + b1g_ref[0]
        lin = _dot(x, w1l_s[...]) + b1l_ref[0]
        glu = jnp.minimum(glu, SWIGLU_LIMIT)
        lin = jnp.clip(lin, -SWIGLU_LIMIT, SWIGLU_LIMIT)
        act = glu * jax.nn.sigmoid(SWIGLU_ALPHA * glu) * (lin + 1.0)
        _store_row_tiles(yb_ref, _dot(act.astype(_BF16), w2_s[...]) + b2_ref[0])

    @pl.when(jnp.logical_not(live))
    def _():
        yb_ref[...] = jnp.zeros_like(yb_ref)


def _experts(xb, block_expert, n_used, next_expert, run_parity, layer, w1, b1, w2, b2):
    n_rows = xb.shape[0] // SUBLANES
    n_blocks = n_rows // MOE_BLOCK
    _, n_e, d, two_f = w1.shape
    row_spec = lambda idx: pl.BlockSpec((MOE_BLOCK * SUBLANES, LANES), lambda b, be, nu, nx, par: (idx(b, nu), 0))
    f = two_f // 2
    group = 2 * LANES
    j = jnp.arange(group)
    src = jnp.where(j < LANES, 2 * j, 2 * (j - LANES) + 1)
    perm = (jnp.arange(group)[:, None] == src[None, :]).astype(_BF16)
    b1p = b1.reshape(n_e, 1, f, 2)
    e_spec = lambda shape: pl.BlockSpec((1,) + shape, lambda b, be, nu, nx, par: (be[b],) + (0,) * len(shape))
    hbm = pl.BlockSpec(memory_space=pl.ANY)
    grid_spec = pltpu.PrefetchScalarGridSpec(
        num_scalar_prefetch=4,
        grid=(n_blocks,),
        in_specs=[row_spec(lambda b, nu: jnp.minimum(b, nu[0] - 1)),
                  hbm, e_spec((1, f)), e_spec((1, f)), hbm, e_spec((1, d)),
                  pl.BlockSpec((group, group), lambda b, be, nu, nx, par: (0, 0))],
        out_specs=row_spec(lambda b, nu: b),
        scratch_shapes=[pltpu.VMEM((2, d, two_f), w1.dtype), pltpu.VMEM((2, f, d), w2.dtype),
                        pltpu.VMEM((d, f), _BF16), pltpu.VMEM((d, f), _BF16), pltpu.VMEM((f, d), _BF16),
                        pltpu.SemaphoreType.DMA((2, 2))],
    )
    return pl.pallas_call(
        functools.partial(_expert_body, layer=layer),
        grid_spec=grid_spec,
        out_shape=jax.ShapeDtypeStruct((n_rows * SUBLANES, LANES), _F32),
        compiler_params=_params("arbitrary"),
        name="moe_experts",
    )(block_expert, n_used, next_expert, run_parity, xb, w1, b1p[..., 0], b1p[..., 1], w2,
      b2.reshape(n_e, 1, d), perm)


def _combine_body(slot_ref, next_slot_ref, x_ref, gate_ref, yb_ref, out_ref, rows_ref, sems):
    i = pl.program_id(0)
    tm = x_ref.shape[0]

    def gather(slots, p):
        def start(r, carry):
            for k in range(TOP_K):
                s = slots[r * TOP_K + k]
                pltpu.make_async_copy(_row_tile(yb_ref, s), _row_tile(rows_ref.at[p, k], r),
                                      sems.at[p]).start(priority=k % 2)
            return carry

        lax.fori_loop(0, tm, start, 0, unroll=2)

    def consume(p):
        for k in range(TOP_K):
            pltpu.make_async_copy(yb_ref.at[pl.ds(0, tm * SUBLANES)], rows_ref.at[p, k], sems.at[p]).wait()
        out_ref[...] = x_ref[...]
        for k in range(TOP_K):
            gate = jnp.broadcast_to(gate_ref[:, k:k + 1], (tm, LANES))
            for j in range(SUBLANES):
                cols = slice(j * LANES, (j + 1) * LANES)
                out_ref[:, cols] += gate * _load_row_tile_column(rows_ref, tm, j, (p, k))

    @pl.when(i == 0)
    def _():
        gather(slot_ref, 0)

    for p in range(2):
        @pl.when(i % 2 == p)
        def _():
            @pl.when(i + 1 < pl.num_programs(0))
            def _():
                gather(next_slot_ref, 1 - p)

            consume(p)


def _combine(x, gates, slot_flat, yb):
    t, d = x.shape
    tm = ROW_MOVE_TILE
    last = t // tm - 1
    return pl.pallas_call(
        _combine_body,
        grid=(t // tm,),
        in_specs=[pl.BlockSpec((tm * TOP_K,), lambda i: (i,), memory_space=pltpu.SMEM),
                  pl.BlockSpec((tm * TOP_K,), lambda i: (jnp.minimum(i + 1, last),), memory_space=pltpu.SMEM),
                  pl.BlockSpec((tm, d), lambda i: (i, 0)),
                  pl.BlockSpec((tm, LANES), lambda i: (i, 0)),
                  pl.BlockSpec(memory_space=pl.ANY)],
        out_specs=pl.BlockSpec((tm, d), lambda i: (i, 0)),
        out_shape=jax.ShapeDtypeStruct((t, d), _F32),
        scratch_shapes=[pltpu.VMEM((2, TOP_K, tm * SUBLANES, LANES), _F32), pltpu.SemaphoreType.DMA((2,))],
        compiler_params=_params("arbitrary"),
        name="moe_combine",
    )(slot_flat, slot_flat, x, gates, yb)


def _moe_blocks(t):
    return -(-(t * TOP_K + N_EXPERTS * (MOE_BLOCK - 1)) // MOE_BLOCK)


def _moe(x, g, w_r, b_r, layer, w1, b1, w2, b2, xb_init):
    t = x.shape[0]
    n_blocks = _moe_blocks(t)
    hp, meta, gates_t, counts = _router(x, g, w_r, b_r)
    counts = counts.astype(jnp.int32)
    padded = (counts + MOE_BLOCK - 1) // MOE_BLOCK * MOE_BLOCK
    pad_end = jnp.cumsum(padded)
    pad_start = pad_end - padded
    experts = jnp.arange(N_EXPERTS, dtype=jnp.int32)
    start_of = jnp.sum(jnp.where(meta[:TOP_K, :, None] == experts, pad_start, 0), axis=-1)
    slot_flat = (start_of + meta[TOP_K:]).T.reshape(-1).astype(jnp.int32)
    gates = _pad_lanes(gates_t[:TOP_K].T, LANES)
    n_used = (pad_end[-1] // MOE_BLOCK).astype(jnp.int32)
    block_id = jnp.minimum(jnp.arange(n_blocks, dtype=jnp.int32), n_used - 1)
    block_expert = jnp.sum(pad_end[None, :] <= (block_id * MOE_BLOCK)[:, None], axis=-1).astype(jnp.int32)
    is_expert = block_expert[:, None] == experts
    next_start = jnp.sum(jnp.where(is_expert, pad_end // MOE_BLOCK, 0), axis=-1)
    next_expert = jnp.sum(pad_end[None, :] <= (next_start * MOE_BLOCK)[:, None], axis=-1).astype(jnp.int32)
    next_expert = jnp.where(next_start < n_used, next_expert, -1)
    run_parity = jnp.sum((counts > 0)[None, :] & (experts < block_expert[:, None]), axis=-1).astype(jnp.int32) % 2
    xb = _dispatch(hp, slot_flat, xb_init)
    yb = _experts(xb, block_expert, n_used.reshape(1), next_expert, run_parity, layer, w1, b1, w2, b2)
    return _combine(x, gates, slot_flat, yb), xb


def kernel(x, mix_norm, ffn_norm, even_w_in, even_w_s, even_b_s, even_g_v, even_g_q, even_g_k, even_w_out, odd_w_in, odd_g_cq, odd_g_ckv, odd_w_uq, odd_w_ukv, odd_g_q, odd_g_k, odd_w_out, router_w, router_b, expert_w1, expert_b1, expert_w2, expert_b2):
    bsz, s, d = x.shape
    xt = x.reshape(bsz * s, d)
    xb = jnp.zeros((_moe_blocks(bsz * s) * MOE_BLOCK * SUBLANES, LANES), _F32)
    for l in range(mix_norm.shape[0]):
        i = l // 2
        if l % 2 == 0:
            a, qkv = _even_in(xt, mix_norm[l], even_w_in[i], even_w_s[i], even_b_s[i], even_g_v[i],
                              even_g_q[i], even_g_k[i], bsz)
            xt = _even_out(xt, a, _dilated_attention(qkv), even_w_out[i])
        else:
            q, k, v = _odd_in(xt, mix_norm[l], odd_w_in[i], odd_g_cq[i], odd_g_ckv[i], odd_w_uq[i],
                              odd_w_ukv[i], odd_g_q[i], odd_g_k[i], bsz)
            o = _flash(q, k, v, FLASH_TQ)
            xt = _proj_out(xt, o.reshape(bsz * s, -1), odd_w_out[i])
        xt, xb = _moe(xt, ffn_norm[l], router_w[l], router_b[l], l, expert_w1, expert_b1[l],
                      expert_w2, expert_b2[l], xb)
    return xt.reshape(bsz, s, d)
```

```python
import functools
import math

import jax
import jax.numpy as jnp
from jax import lax
from jax.experimental import pallas as pl
from jax.experimental.pallas import tpu as pltpu

EPS = 1e-6
NEG_INF = -1e30
A_HEADS = 4
A_HEAD_DIM = 128
A_CHUNK = 128
A_WIDTH = A_HEADS * A_HEAD_DIM
B_HEADS = 8
B_HEAD_DIM = 64
B_WIDTH = B_HEADS * B_HEAD_DIM
B_BRANCHES = ((128, 1), (512, 4), (2048, 16))
B_BLOCK = 128
C_HEADS = 8
C_NOPE = 128
C_ROPE = 64
C_V = 128
C_Q_LORA = 512
C_KV_LORA = 256
C_QK_PAD = 256
ROPE_THETA = 10000.0
N_EXPERTS = 32
TOP_K = 4
D_EXPERT = 1024
SWIGLU_ALPHA = 1.702
SWIGLU_LIMIT = 7.0
MOE_BLOCK = 512

LANES = 128
TOKEN_TILE = 256
ROW_MOVE_TILE = 512
VMEM_LIMIT = 56 * 1024 * 1024

_F32 = jnp.float32
_BF16 = jnp.bfloat16


def _params(*sem):
    return pltpu.CompilerParams(dimension_semantics=sem, vmem_limit_bytes=VMEM_LIMIT)


def _rms(x, g):
    return x * lax.rsqrt(jnp.mean(x * x, axis=-1, keepdims=True) + EPS) * g


def _gelu(x):
    return 0.5 * x * (1.0 + lax.erf(x * math.sqrt(0.5)))


def _dot(a, b):
    return jnp.dot(a, b, preferred_element_type=_F32)


def _dot_nt(a, b):
    return lax.dot_general(a, b, (((1,), (1,)), ((), ())), preferred_element_type=_F32)


def _seg_sum(x2, seg):
    hi = x2.astype(_BF16)
    lo = (x2 - hi.astype(_F32)).astype(_BF16)
    return _dot(hi, seg) + _dot(lo, seg)


def _const_spec(shape):
    nd = len(shape)
    return pl.BlockSpec(shape, lambda *_: (0,) * nd)


def _even_in_body(x_ref, g_ref, win_ref, ws_ref, bst_ref, gv_ref, gq_ref, gk_ref, seg_ref,
                  a_ref, *rest):
    qkv_refs, stage_ref = rest[:-1], rest[-1]
    x = x_ref[0]
    h = _rms(x, g_ref[...]).astype(_BF16)
    z = _dot(h, win_ref[...])
    tm = x.shape[0]
    u = _gelu(z[:, :A_WIDTH])
    vv = _gelu(z[:, A_WIDTH:2 * A_WIDTH])
    row = lax.broadcasted_iota(jnp.int32, (A_CHUNK, A_CHUNK), 0)
    col = lax.broadcasted_iota(jnp.int32, (A_CHUNK, A_CHUNK), 1)
    for hd in range(A_HEADS):
        sl = slice(hd * A_HEAD_DIM, (hd + 1) * A_HEAD_DIM)
        vh = _rms(vv[:, sl], gv_ref[hd:hd + 1, :]).astype(_BF16)
        w = jnp.where(row >= col, ws_ref[hd], 0.0).astype(_BF16)
        bias = bst_ref[:, hd:hd + 1]
        for c in range(tm // A_CHUNK):
            rs = slice(c * A_CHUNK, (c + 1) * A_CHUNK)
            mixed = _dot(w, vh[rs]) + bias
            a_ref[0, rs, sl] = (u[rs, sl] * mixed).astype(a_ref.dtype)
    o = 2 * A_WIDTH
    q = z[:, o:o + B_WIDTH]
    k = z[:, o + B_WIDTH:o + 2 * B_WIDTH]
    seg = seg_ref[...]
    inv = 1.0 / B_HEAD_DIM
    qn = q * lax.rsqrt(_seg_sum(q * q, seg) * inv + EPS) * gq_ref[...]
    kn = k * lax.rsqrt(_seg_sum(k * k, seg) * inv + EPS) * gk_ref[...]
    qkv = (qn * (B_HEAD_DIM ** -0.5), kn, z[:, o + 2 * B_WIDTH:])
    groups = B_WIDTH // LANES
    for which, val in enumerate(qkv):
        for gi in range(groups):
            stage_ref[gi] = val[:, gi * LANES:(gi + 1) * LANES]
        for bi, (_, dil) in enumerate(B_BRANCHES):
            ref = qkv_refs[3 * bi + which]
            if dil == 1:
                ref[0] = val.astype(ref.dtype)
                continue
            for r in range(dil):
                for gi in range(groups):
                    rows = stage_ref[gi, pl.ds(r, tm // dil, stride=dil), :]
                    lo = r * B_WIDTH + gi * LANES
                    ref[0, :, lo:lo + LANES] = rows.astype(ref.dtype)


def _class_view_spec(tm, dil):
    return pl.BlockSpec((1, tm // dil, dil * B_WIDTH), lambda b, j: (b, j, 0))


def _even_in(x, g, w_in, w_s, b_s, g_v, g_q, g_k, bsz):
    t, d = x.shape
    s = t // bsz
    tm = TOKEN_TILE
    seg = (jnp.arange(B_WIDTH)[:, None] // B_HEAD_DIM == jnp.arange(B_WIDTH)[None, :] // B_HEAD_DIM).astype(_BF16)
    tile = lambda w: pl.BlockSpec((1, tm, w), lambda b, j: (b, j, 0))
    qkv_specs, qkv_shapes = [], []
    for _, dil in B_BRANCHES:
        qkv_specs += [_class_view_spec(tm, dil)] * 3
        qkv_shapes += [jax.ShapeDtypeStruct((bsz, s // dil, dil * B_WIDTH), _BF16)] * 3
    outs = pl.pallas_call(
        _even_in_body,
        grid=(bsz, s // tm),
        in_specs=[tile(d), _const_spec((1, d)), _const_spec(w_in.shape), _const_spec(w_s.shape),
                  _const_spec((A_CHUNK, A_HEADS)), _const_spec(g_v.shape), _const_spec((1, B_WIDTH)),
                  _const_spec((1, B_WIDTH)), _const_spec(seg.shape)],
        out_specs=[tile(A_WIDTH)] + qkv_specs,
        out_shape=[jax.ShapeDtypeStruct((bsz, s, A_WIDTH), _BF16)] + qkv_shapes,
        scratch_shapes=[pltpu.VMEM((B_WIDTH // LANES, tm, LANES), _F32)],
        compiler_params=_params("parallel", "parallel"),
        name="even_in",
    )(x.reshape(bsz, s, d), g.reshape(1, d), w_in.astype(_BF16), w_s, b_s.T, g_v,
      jnp.tile(g_q, B_HEADS).reshape(1, B_WIDTH), jnp.tile(g_k, B_HEADS).reshape(1, B_WIDTH), seg)
    return outs[0], [outs[1 + 3 * bi:4 + 3 * bi] for bi in range(len(B_BRANCHES))]


def _dilated_body(*refs, blocks_per_class):
    nb = len(blocks_per_class)
    t = pl.program_id(1)
    for bi, n in enumerate(blocks_per_class):
        _dilated_block(*refs[5 * bi:5 * bi + 5], *refs[5 * nb + 2 * bi:5 * nb + 2 * bi + 2], t % n)


def _dilated_block(q_ref, kp_ref, kc_ref, vp_ref, vc_ref, o_ref, lse_ref, jb):
    q = q_ref[0]
    kk = jnp.concatenate([kp_ref[0], kc_ref[0]], axis=0)
    vv = jnp.concatenate([vp_ref[0], vc_ref[0]], axis=0)
    i = lax.broadcasted_iota(jnp.int32, (B_BLOCK, 2 * B_BLOCK), 0)
    c = lax.broadcasted_iota(jnp.int32, (B_BLOCK, 2 * B_BLOCK), 1)
    valid = (c >= i) & (c <= i + B_BLOCK) & ((jb > 0) | (c >= B_BLOCK))
    lane = lax.broadcasted_iota(jnp.int32, (1, LANES), 1)
    first = lane < B_HEAD_DIM
    for p in range(B_WIDTH // LANES):
        sl = slice(p * LANES, (p + 1) * LANES)
        q2, k2, v2 = q[:, sl], kk[:, sl], vv[:, sl]
        outs, lses = [], []
        for half in (first, jnp.logical_not(first)):
            s = _dot_nt(jnp.where(half, q2, jnp.zeros_like(q2)), k2)
            s = jnp.where(valid, s, NEG_INF)
            m = jnp.max(s, axis=-1, keepdims=True)
            e = jnp.exp(s - m)
            l = jnp.sum(e, axis=-1, keepdims=True)
            outs.append(_dot(e.astype(_BF16), v2) / l)
            lses.append(m + jnp.log(l))
        o_ref[0, :, sl] = jnp.where(first, outs[0], outs[1])
        lse_ref[0, :, sl] = jnp.where(first, lses[0], lses[1])


def _dilated_attention(qkv):
    bsz, s, _ = qkv[0][0].shape
    steps = s // B_BLOCK
    in_specs, out_specs, out_shapes, args, blocks_per_class = [], [], [], [], []
    for (q, k, v), (_, dil) in zip(qkv, B_BRANCHES):
        n = steps // dil
        cur = pl.BlockSpec((1, B_BLOCK, B_WIDTH), lambda b, t, n=n: (b, t % n, t // n))
        prev = pl.BlockSpec((1, B_BLOCK, B_WIDTH), lambda b, t, n=n: (b, jnp.maximum(t % n - 1, 0), t // n))
        in_specs += [cur, prev, cur, prev, cur]
        out_specs += [cur, cur]
        out_shapes += [jax.ShapeDtypeStruct(q.shape, _F32)] * 2
        args += [q, k, k, v, v]
        blocks_per_class.append(n)
    outs = pl.pallas_call(
        functools.partial(_dilated_body, blocks_per_class=tuple(blocks_per_class)),
        grid=(bsz, steps),
        in_specs=in_specs,
        out_specs=out_specs,
        out_shape=out_shapes,
        compiler_params=_params("parallel", "parallel"),
        name="dilated_attention",
    )(*args)
    return [(outs[2 * bi], outs[2 * bi + 1]) for bi in range(len(B_BRANCHES))]


def _even_out_body(x_ref, a_ref, *rest):
    nb = len(B_BRANCHES)
    branch_refs, w_ref, out_ref, stage_ref = rest[:2 * nb], rest[2 * nb], rest[2 * nb + 1], rest[2 * nb + 2]
    tm = x_ref.shape[1]

    def token_order(ref, dil, slot):
        if dil == 1:
            return ref[0]
        groups = B_WIDTH // LANES
        for r in range(dil):
            for gi in range(groups):
                lo = r * B_WIDTH + gi * LANES
                stage_ref[slot, gi, pl.ds(r, tm // dil, stride=dil), :] = ref[0, :, lo:lo + LANES]
        return jnp.concatenate([stage_ref[slot, gi] for gi in range(groups)], axis=1)

    outs = [token_order(branch_refs[2 * bi], dil, 2 * bi) for bi, (_, dil) in enumerate(B_BRANCHES)]
    lses = [token_order(branch_refs[2 * bi + 1], dil, 2 * bi + 1) for bi, (_, dil) in enumerate(B_BRANCHES)]
    m = functools.reduce(jnp.maximum, lses)
    es = [jnp.exp(l - m) for l in lses]
    b = sum(e * o for e, o in zip(es, outs)) / sum(es)
    y = _dot(a_ref[0], w_ref[:A_WIDTH, :]) + _dot(b.astype(_BF16), w_ref[A_WIDTH:, :])
    out_ref[0] = x_ref[0] + y


def _even_out(x, a, branches, w_out):
    t, d = x.shape
    bsz, s, _ = a.shape
    tm = TOKEN_TILE
    tile = lambda w: pl.BlockSpec((1, tm, w), lambda b, j: (b, j, 0))
    branch_specs = []
    for _, dil in B_BRANCHES:
        branch_specs += [_class_view_spec(tm, dil)] * 2
    out = pl.pallas_call(
        _even_out_body,
        grid=(bsz, s // tm),
        in_specs=[tile(d), tile(A_WIDTH)] + branch_specs + [_const_spec(w_out.shape)],
        out_specs=tile(d),
        out_shape=jax.ShapeDtypeStruct((bsz, s, d), _F32),
        scratch_shapes=[pltpu.VMEM((2 * len(B_BRANCHES), B_WIDTH // LANES, tm, LANES), _F32)],
        compiler_params=_params("parallel", "parallel"),
        name="even_out",
    )(x.reshape(bsz, s, d), a, *[arr for pair in branches for arr in pair], w_out.astype(_BF16))
    return out.reshape(t, d)


def _rope(x, cos, sin_signed, low):
    partner = jnp.where(low, pltpu.roll(x, LANES - C_ROPE // 2, 1), pltpu.roll(x, C_ROPE // 2, 1))
    return x * cos + partner * sin_signed


def _odd_in_body(x_ref, g_ref, win_ref, gcq_ref, gckv_ref, wqn_ref, wqr_ref, wkn_ref, wv_ref,
                 gqn_ref, gqr_ref, gkn_ref, gkr_ref, cos_ref, sin_ref, q_ref, k_ref, v_ref):
    x = x_ref[0]
    h = _rms(x, g_ref[...]).astype(_BF16)
    z = _dot(h, win_ref[...])
    cq = _rms(z[:, :C_Q_LORA], gcq_ref[...]).astype(_BF16)
    ckv = _rms(z[:, C_Q_LORA:C_Q_LORA + C_KV_LORA], gckv_ref[...]).astype(_BF16)
    kr = z[:, C_Q_LORA + C_KV_LORA:]
    qn = _dot(cq, wqn_ref[...])
    qr = _dot(cq, wqr_ref[...])
    kn = _dot(ckv, wkn_ref[...])
    vv = _dot(ckv, wv_ref[...])
    cos, sin = cos_ref[...], sin_ref[...]
    lane = lax.broadcasted_iota(jnp.int32, (1, LANES), 1)
    low = (lane % C_ROPE) < (C_ROPE // 2)
    inv = 1.0 / (C_NOPE + C_ROPE)
    scale = (C_NOPE + C_ROPE) ** -0.5 * math.log2(math.e)
    kr_ss = jnp.sum(kr * kr, axis=-1, keepdims=True)
    kr_roped = _rope(kr * gkr_ref[...], cos, sin, low)
    for hd in range(C_HEADS):
        sl = slice(hd * LANES, (hd + 1) * LANES)
        qnh, qrh, knh = qn[:, sl], qr[:, sl], kn[:, sl]
        ssq = jnp.sum(qnh * qnh, axis=-1, keepdims=True) + jnp.sum(qrh * qrh, axis=-1, keepdims=True)
        rq = lax.rsqrt(ssq * inv + EPS) * scale
        q_ref[0, hd, :, :LANES] = (qnh * rq * gqn_ref[...]).astype(q_ref.dtype)
        q_ref[0, hd, :, LANES:] = (_rope(qrh * gqr_ref[...], cos, sin, low) * rq).astype(q_ref.dtype)
        ssk = jnp.sum(knh * knh, axis=-1, keepdims=True) + kr_ss
        rk = lax.rsqrt(ssk * inv + EPS)
        k_ref[0, hd, :, :LANES] = (knh * rk * gkn_ref[...]).astype(k_ref.dtype)
        k_ref[0, hd, :, LANES:] = (kr_roped * rk).astype(k_ref.dtype)
        v_ref[0, hd, :, :C_V] = vv[:, sl].astype(v_ref.dtype)
        v_ref[0, hd, :, C_V:] = jnp.ones((x.shape[0], C_V), v_ref.dtype)


def _pad_lanes(a, width):
    return jnp.pad(a, [(0, 0)] * (a.ndim - 1) + [(0, width - a.shape[-1])])


def _odd_in(x, g, w_in, g_cq, g_ckv, w_uq, w_ukv, g_q, g_k, bsz):
    t, d = x.shape
    s = t // bsz
    tm = TOKEN_TILE
    w_in_p = _pad_lanes(w_in, C_Q_LORA + C_KV_LORA + LANES).astype(_BF16)
    wq = w_uq.reshape(C_Q_LORA, C_HEADS, C_NOPE + C_ROPE)
    wqn = wq[:, :, :C_NOPE].reshape(C_Q_LORA, C_HEADS * C_NOPE).astype(_BF16)
    wqr = _pad_lanes(wq[:, :, C_NOPE:], LANES).reshape(C_Q_LORA, C_HEADS * LANES).astype(_BF16)
    wkv = w_ukv.reshape(C_KV_LORA, C_HEADS, C_NOPE + C_V)
    wkn = wkv[:, :, :C_NOPE].reshape(C_KV_LORA, C_HEADS * C_NOPE).astype(_BF16)
    wv = wkv[:, :, C_NOPE:].reshape(C_KV_LORA, C_HEADS * C_V).astype(_BF16)
    half = C_ROPE // 2
    inv_freq = ROPE_THETA ** (-jnp.arange(half, dtype=_F32) / half)
    ang = jnp.arange(s, dtype=_F32)[:, None] * inv_freq[None, :]
    cos = jnp.tile(jnp.cos(ang), (1, LANES // half))
    sin = jnp.tile(jnp.concatenate([-jnp.sin(ang), jnp.sin(ang)], axis=-1), (1, LANES // C_ROPE))
    row = lambda a: a.reshape(1, -1)
    qk_shape = jax.ShapeDtypeStruct((bsz, C_HEADS, s, C_QK_PAD), _BF16)
    head_spec = lambda w: pl.BlockSpec((1, C_HEADS, tm, w), lambda b, j: (b, 0, j, 0))
    pos_spec = pl.BlockSpec((tm, LANES), lambda b, j: (j, 0))
    consts = [row(g), w_in_p, row(g_cq), row(g_ckv), wqn, wqr, wkn, wv,
              row(g_q[:C_NOPE]), row(_pad_lanes(g_q[C_NOPE:], LANES)),
              row(g_k[:C_NOPE]), row(_pad_lanes(g_k[C_NOPE:], LANES))]
    return pl.pallas_call(
        _odd_in_body,
        grid=(bsz, s // tm),
        in_specs=[pl.BlockSpec((1, tm, d), lambda b, j: (b, j, 0))]
        + [_const_spec(c.shape) for c in consts] + [pos_spec, pos_spec],
        out_specs=[head_spec(C_QK_PAD), head_spec(C_QK_PAD), head_spec(2 * C_V)],
        out_shape=[qk_shape, qk_shape, jax.ShapeDtypeStruct((bsz, C_HEADS, s, 2 * C_V), _BF16)],
        compiler_params=_params("parallel", "parallel"),
        name="odd_in",
    )(x.reshape(bsz, s, d), *consts, cos, sin)


FLASH_ROWS = 128
FLASH_TQ = 1024


def _flash_body(q_ref, k_ref, v_ref, o_ref, s_ref, m_ref, acc_ref):
    qi = pl.program_id(2)
    tq = q_ref.shape[2]
    tk = tq
    dv = o_ref.shape[2]
    m_ref[...] = jnp.full_like(m_ref, NEG_INF)
    acc_ref[...] = jnp.zeros_like(acc_ref)

    def scores(j):
        start = pl.multiple_of(j * tk, tk)
        return _dot_nt(q_ref[0, 0], k_ref[0, 0, pl.ds(start, tk), :])

    def softmax_pv(s_all, j, diagonal):
        start = pl.multiple_of(j * tk, tk)
        v = v_ref[0, 0, pl.ds(start, tk), :]
        if diagonal:
            r = lax.broadcasted_iota(jnp.int32, s_all.shape, 0)
            col = lax.broadcasted_iota(jnp.int32, s_all.shape, 1)
            s_all = jnp.where(col <= r, s_all, NEG_INF)
        ps, scales = [], []
        for c in range(tq // FLASH_ROWS):
            rows = slice(c * FLASH_ROWS, (c + 1) * FLASH_ROWS)
            groups = [s_all[rows, g * LANES:(g + 1) * LANES] for g in range(tk // LANES)]
            m_old = m_ref[rows]
            gmax = functools.reduce(jnp.maximum, groups)
            m_new = jnp.maximum(m_old, jnp.max(gmax, axis=-1, keepdims=True))
            alpha = jnp.exp2(m_old - m_new)
            ps.append(jnp.concatenate([jnp.exp2((g - m_new).astype(_BF16)) for g in groups], axis=1))
            scales.append(jnp.concatenate([alpha] * (acc_ref.shape[1] // LANES), axis=1))
            m_ref[rows] = m_new
        pv = _dot(jnp.concatenate(ps, axis=0), v)
        acc_ref[...] = jnp.concatenate(scales, axis=0) * acc_ref[...] + pv

    s_ref[...] = scores(0)

    def full_tile(j, carry):
        s_cur = s_ref[...]
        s_next = scores(j + 1)
        softmax_pv(s_cur, j, False)
        s_ref[...] = s_next
        return carry

    lax.fori_loop(0, qi, full_tile, 0)
    softmax_pv(s_ref[...], qi, True)
    acc = acc_ref[...]
    o_ref[0] = (acc[:, :dv] / acc[:, dv:2 * dv]).astype(o_ref.dtype)


def _flash(q, k, v, tq):
    bsz, nh, s, dq = q.shape
    dv2 = v.shape[-1]
    dv = dv2 // 2
    whole = lambda w: pl.BlockSpec((1, 1, s, w), lambda b, h, i: (b, h, 0, 0))
    return pl.pallas_call(
        _flash_body,
        grid=(bsz, nh, s // tq),
        in_specs=[pl.BlockSpec((1, 1, tq, dq), lambda b, h, i: (b, h, i, 0)), whole(dq), whole(dv2)],
        out_specs=pl.BlockSpec((1, tq, dv), lambda b, h, i: (b, i, h)),
        out_shape=jax.ShapeDtypeStruct((bsz, s, nh * dv), _BF16),
        scratch_shapes=[pltpu.VMEM((tq, tq), _F32), pltpu.VMEM((tq, LANES), _F32), pltpu.VMEM((tq, dv2), _F32)],
        compiler_params=_params("parallel", "parallel", "arbitrary"),
        name="mla_flash",
    )(q, k, v)


def _proj_out_body(x_ref, o_ref, w_ref, out_ref):
    out_ref[...] = x_ref[...] + _dot(o_ref[...], w_ref[...])


def _proj_out(x, o, w_out):
    t, d = x.shape
    tm = TOKEN_TILE
    tile = lambda w: pl.BlockSpec((tm, w), lambda i: (i, 0))
    return pl.pallas_call(
        _proj_out_body,
        grid=(t // tm,),
        in_specs=[tile(d), tile(o.shape[1]), _const_spec(w_out.shape)],
        out_specs=tile(d),
        out_shape=jax.ShapeDtypeStruct((t, d), _F32),
        compiler_params=_params("parallel"),
        name="odd_out",
    )(x, o, w_out.astype(_BF16))


SUBLANES = 8


def _store_row_tiles(ref, val, lead=()):
    n = val.shape[0]
    for j in range(SUBLANES):
        ref[lead + (pl.ds(j, n, stride=SUBLANES), slice(None))] = val[:, j * LANES:(j + 1) * LANES]


def _load_row_tile_column(ref, n, j, lead=()):
    return ref[lead + (pl.ds(j, n, stride=SUBLANES), slice(None))]


def _row_tile(ref, row):
    return ref.at[pl.ds(pl.multiple_of(row * SUBLANES, SUBLANES), SUBLANES)]


def _router_body(x_ref, g_ref, wt_ref, b_ref, hp_ref, meta_ref, gate_ref, cnt_ref, base_ref):
    @pl.when(pl.program_id(0) == 0)
    def _():
        base_ref[...] = jnp.zeros_like(base_ref)

    h = _rms(x_ref[...], g_ref[...])
    tm = h.shape[0]
    _store_row_tiles(hp_ref, h)
    w = wt_ref[...]
    h_hi, w_hi = h.astype(_BF16), w.astype(_BF16)
    h_lo = (h - h_hi.astype(_F32)).astype(_BF16)
    w_lo = (w - w_hi.astype(_F32)).astype(_BF16)
    logits = _dot_nt(w_hi, h_hi) + _dot_nt(w_hi, h_lo) + _dot_nt(w_lo, h_hi) + b_ref[...]

    expert = lax.broadcasted_iota(jnp.int32, logits.shape, 0)
    hits, sels, tops = [], [], []
    work = logits
    for k in range(TOP_K):
        m = jnp.max(work, axis=0, keepdims=True)
        sel = jnp.min(jnp.where(work == m, expert, N_EXPERTS), axis=0, keepdims=True)
        hit = expert == sel
        work = jnp.where(hit, -jnp.inf, work)
        hits.append(hit)
        sels.append(sel)
        tops.append(m)
    r = lax.broadcasted_iota(jnp.int32, (tm, tm), 0)
    c = lax.broadcasted_iota(jnp.int32, (tm, tm), 1)
    earlier_tokens = (r < c).astype(_BF16)
    stacked = jnp.concatenate([hit.astype(_BF16) for hit in hits], axis=0)
    cum = _dot(stacked, earlier_tokens)
    base = base_ref[...][:, :1]
    ranks = []
    for k in range(TOP_K):
        onehot = hits[k].astype(_F32)
        before = cum[k * N_EXPERTS:(k + 1) * N_EXPERTS] + base
        ranks.append(jnp.sum(onehot * before, axis=0, keepdims=True).astype(jnp.int32))
        base = base + jnp.sum(onehot, axis=1, keepdims=True)
    base_ref[...] = jnp.broadcast_to(base, base_ref.shape)
    cnt_ref[...] = jnp.broadcast_to(base, cnt_ref.shape)
    exps = [jnp.ones_like(tops[0])] + [jnp.exp(tops[k] - tops[0]) for k in range(1, TOP_K)]
    denom = functools.reduce(lambda a_, b_: a_ + b_, exps)
    meta_ref[...] = jnp.concatenate(sels + ranks, axis=0)
    gate_ref[...] = jnp.concatenate([e / denom for e in exps] + [jnp.zeros_like(denom)] * TOP_K, axis=0)


def _router(x, g, w_r, b_r):
    t, d = x.shape
    tm = TOKEN_TILE
    rows = 2 * TOP_K
    lane_tile = pl.BlockSpec((rows, tm), lambda i: (0, i))
    hp, meta, gates, counts = pl.pallas_call(
        _router_body,
        grid=(t // tm,),
        in_specs=[pl.BlockSpec((tm, d), lambda i: (i, 0)), _const_spec((1, d)), _const_spec((N_EXPERTS, d)),
                  _const_spec((N_EXPERTS, 1))],
        out_specs=[pl.BlockSpec((tm * SUBLANES, LANES), lambda i: (i, 0)), lane_tile, lane_tile,
                   _const_spec((N_EXPERTS, LANES))],
        out_shape=[jax.ShapeDtypeStruct((t * SUBLANES, LANES), _F32), jax.ShapeDtypeStruct((rows, t), jnp.int32),
                   jax.ShapeDtypeStruct((rows, t), _F32), jax.ShapeDtypeStruct((N_EXPERTS, LANES), _F32)],
        scratch_shapes=[pltpu.VMEM((N_EXPERTS, LANES), _F32)],
        compiler_params=_params("arbitrary"),
        name="moe_router",
    )(x, g.reshape(1, d), w_r.T, b_r.reshape(N_EXPERTS, 1))
    return hp, meta, gates, counts[:, 0]


def _dispatch_body(slot_ref, hp_ref, xb_in_ref, xb_ref, sem):
    del xb_in_ref
    tm = hp_ref.shape[0] // SUBLANES

    def start(r, carry):
        for k in range(TOP_K):
            s = slot_ref[r * TOP_K + k]
            pltpu.make_async_copy(_row_tile(hp_ref, r), _row_tile(xb_ref, s), sem).start(priority=k % 2)
        return carry

    lax.fori_loop(0, tm, start, 0, unroll=8)
    for _ in range(TOP_K):
        pltpu.make_async_copy(hp_ref, xb_ref.at[pl.ds(0, tm * SUBLANES)], sem).wait()


def _dispatch(hp, slot_flat, xb_init):
    t = hp.shape[0] // SUBLANES
    tm = ROW_MOVE_TILE
    return pl.pallas_call(
        _dispatch_body,
        grid=(t // tm,),
        in_specs=[pl.BlockSpec((tm * TOP_K,), lambda i: (i,), memory_space=pltpu.SMEM),
                  pl.BlockSpec((tm * SUBLANES, LANES), lambda i: (i, 0)),
                  pl.BlockSpec(memory_space=pl.ANY)],
        out_specs=pl.BlockSpec(memory_space=pl.ANY),
        out_shape=jax.ShapeDtypeStruct(xb_init.shape, hp.dtype),
        scratch_shapes=[pltpu.SemaphoreType.DMA(())],
        input_output_aliases={2: 0},
        compiler_params=_params("arbitrary"),
        name="moe_dispatch",
    )(slot_flat, hp, xb_init)


def _expert_body(be_ref, nu_ref, nx_ref, par_ref, xb_ref, w1_hbm, b1g_ref, b1l_ref, w2_hbm, b2_ref, perm_ref,
                 yb_ref, w1_buf, w2_buf, w1g_s, w1l_s, w2_s, sems, *, layer):
    b = pl.program_id(0)
    live = b < nu_ref[0]
    changed = (b == 0) | (be_ref[b] != be_ref[jnp.maximum(b - 1, 0)])

    def fetch(e, slot):
        return (pltpu.make_async_copy(w1_hbm.at[layer, e], w1_buf.at[slot], sems.at[slot, 0]),
                pltpu.make_async_copy(w2_hbm.at[layer, e], w2_buf.at[slot], sems.at[slot, 1]))

    @pl.when(b == 0)
    def _():
        for copy in fetch(be_ref[0], 0):
            copy.start()

    @pl.when(live & changed)
    def _():
        slot = par_ref[b]
        for copy in fetch(be_ref[b], slot):
            copy.wait()

        @pl.when(nx_ref[b] >= 0)
        def _():
            for copy in fetch(nx_ref[b], 1 - slot):
                copy.start()

        group = perm_ref.shape[0]
        for c in range(w1_buf.shape[2] // group):
            wc = w1_buf[slot, :, c * group:(c + 1) * group].astype(_BF16)
            d = _dot(wc, perm_ref[...]).astype(_BF16)
            w1g_s[:, c * (group // 2):(c + 1) * (group // 2)] = d[:, :group // 2]
            w1l_s[:, c * (group // 2):(c + 1) * (group // 2)] = d[:, group // 2:]
        w2_s[...] = w2_buf[slot].astype(_BF16)

    @pl.when(live)
    def _():
        x = jnp.concatenate([_load_row_tile_column(xb_ref, MOE_BLOCK, j) for j in range(SUBLANES)],
                            axis=1).astype(_BF16)
        glu = _dot(x, w1g_s[...]) + b1g_ref[0]
        lin = _dot(x, w1l_s[...]) + b1l_ref[0]
        glu = jnp.minimum(glu, SWIGLU_LIMIT)
        lin = jnp.clip(lin, -SWIGLU_LIMIT, SWIGLU_LIMIT)
        act = glu * jax.nn.sigmoid(SWIGLU_ALPHA * glu) * (lin + 1.0)
        _store_row_tiles(yb_ref, _dot(act.astype(_BF16), w2_s[...]) + b2_ref[0])

    @pl.when(jnp.logical_not(live))
    def _():
        yb_ref[...] = jnp.zeros_like(yb_ref)


def _experts(xb, block_expert, n_used, next_expert, run_parity, layer, w1, b1, w2, b2):
    n_rows = xb.shape[0] // SUBLANES
    n_blocks = n_rows // MOE_BLOCK
    _, n_e, d, two_f = w1.shape
    row_spec = lambda idx: pl.BlockSpec((MOE_BLOCK * SUBLANES, LANES), lambda b, be, nu, nx, par: (idx(b, nu), 0))
    f = two_f // 2
    group = 2 * LANES
    j = jnp.arange(group)
    src = jnp.where(j < LANES, 2 * j, 2 * (j - LANES) + 1)
    perm = (jnp.arange(group)[:, None] == src[None, :]).astype(_BF16)
    b1p = b1.reshape(n_e, 1, f, 2)
    e_spec = lambda shape: pl.BlockSpec((1,) + shape, lambda b, be, nu, nx, par: (be[b],) + (0,) * len(shape))
    hbm = pl.BlockSpec(memory_space=pl.ANY)
    grid_spec = pltpu.PrefetchScalarGridSpec(
        num_scalar_prefetch=4,
        grid=(n_blocks,),
        in_specs=[row_spec(lambda b, nu: jnp.minimum(b, nu[0] - 1)),
                  hbm, e_spec((1, f)), e_spec((1, f)), hbm, e_spec((1, d)),
                  pl.BlockSpec((group, group), lambda b, be, nu, nx, par: (0, 0))],
        out_specs=row_spec(lambda b, nu: b),
        scratch_shapes=[pltpu.VMEM((2, d, two_f), w1.dtype), pltpu.VMEM((2, f, d), w2.dtype),
                        pltpu.VMEM((d, f), _BF16), pltpu.VMEM((d, f), _BF16), pltpu.VMEM((f, d), _BF16),
                        pltpu.SemaphoreType.DMA((2, 2))],
    )
    return pl.pallas_call(
        functools.partial(_expert_body, layer=layer),
        grid_spec=grid_spec,
        out_shape=jax.ShapeDtypeStruct((n_rows * SUBLANES, LANES), _F32),
        compiler_params=_params("arbitrary"),
        name="moe_experts",
    )(block_expert, n_used, next_expert, run_parity, xb, w1, b1p[..., 0], b1p[..., 1], w2,
      b2.reshape(n_e, 1, d), perm)


def _combine_body(slot_ref, next_slot_ref, x_ref, gate_ref, yb_ref, out_ref, rows_ref, sems):
    i = pl.program_id(0)
    tm = x_ref.shape[0]
    groups = tm // SUBLANES

    def start_group(slots, p, g):
        for u in range(SUBLANES):
            r = g * SUBLANES + u
            for k in range(TOP_K):
                s = slots[r * TOP_K + k]
                pltpu.make_async_copy(_row_tile(yb_ref, s), _row_tile(rows_ref.at[p, k], r),
                                      sems.at[p]).start(priority=k % 2)

    def sum_group(p, g):
        t0 = pl.multiple_of(g * SUBLANES, SUBLANES)
        tok = pl.ds(t0, SUBLANES)
        gates = [jnp.broadcast_to(gate_ref[tok, k:k + 1], (SUBLANES, LANES)) for k in range(TOP_K)]
        for j in range(SUBLANES):
            cols = slice(j * LANES, (j + 1) * LANES)
            acc = x_ref[tok, cols]
            for k in range(TOP_K):
                acc = acc + gates[k] * rows_ref[p, k, pl.ds(t0 * SUBLANES + j, SUBLANES, stride=SUBLANES), :]
            out_ref[tok, cols] = acc

    def loop(body):
        def trip(g, carry):
            body(g)
            return carry

        lax.fori_loop(0, groups, trip, 0)

    @pl.when(i == 0)
    def _():
        loop(lambda g: start_group(slot_ref, 0, g))

    for p in range(2):
        @pl.when(i % 2 == p)
        def _():
            for k in range(TOP_K):
                pltpu.make_async_copy(yb_ref.at[pl.ds(0, tm * SUBLANES)], rows_ref.at[p, k], sems.at[p]).wait()

            @pl.when(i + 1 < pl.num_programs(0))
            def _():
                def both(g):
                    start_group(next_slot_ref, 1 - p, g)
                    sum_group(p, g)

                loop(both)

            @pl.when(i + 1 == pl.num_programs(0))
            def _():
                loop(lambda g: sum_group(p, g))


def _combine(x, gates, slot_flat, yb):
    t, d = x.shape
    tm = ROW_MOVE_TILE
    last = t // tm - 1
    return pl.pallas_call(
        _combine_body,
        grid=(t // tm,),
        in_specs=[pl.BlockSpec((tm * TOP_K,), lambda i: (i,), memory_space=pltpu.SMEM),
                  pl.BlockSpec((tm * TOP_K,), lambda i: (jnp.minimum(i + 1, last),), memory_space=pltpu.SMEM),
                  pl.BlockSpec((tm, d), lambda i: (i, 0)),
                  pl.BlockSpec((tm, LANES), lambda i: (i, 0)),
                  pl.BlockSpec(memory_space=pl.ANY)],
        out_specs=pl.BlockSpec((tm, d), lambda i: (i, 0)),
        out_shape=jax.ShapeDtypeStruct((t, d), _F32),
        scratch_shapes=[pltpu.VMEM((2, TOP_K, tm * SUBLANES, LANES), _F32), pltpu.SemaphoreType.DMA((2,))],
        compiler_params=_params("arbitrary"),
        name="moe_combine",
    )(slot_flat, slot_flat, x, gates, yb)


def _moe_blocks(t):
    return -(-(t * TOP_K + N_EXPERTS * (MOE_BLOCK - 1)) // MOE_BLOCK)


def _moe(x, g, w_r, b_r, layer, w1, b1, w2, b2, xb_init):
    t = x.shape[0]
    n_blocks = _moe_blocks(t)
    hp, meta, gates_t, counts = _router(x, g, w_r, b_r)
    counts = counts.astype(jnp.int32)
    padded = (counts + MOE_BLOCK - 1) // MOE_BLOCK * MOE_BLOCK
    pad_end = jnp.cumsum(padded)
    pad_start = pad_end - padded
    experts = jnp.arange(N_EXPERTS, dtype=jnp.int32)
    start_of = jnp.sum(jnp.where(meta[:TOP_K, :, None] == experts, pad_start, 0), axis=-1)
    slot_flat = (start_of + meta[TOP_K:]).T.reshape(-1).astype(jnp.int32)
    gates = _pad_lanes(gates_t[:TOP_K].T, LANES)
    n_used = (pad_end[-1] // MOE_BLOCK).astype(jnp.int32)
    block_id = jnp.minimum(jnp.arange(n_blocks, dtype=jnp.int32), n_used - 1)
    block_expert = jnp.sum(pad_end[None, :] <= (block_id * MOE_BLOCK)[:, None], axis=-1).astype(jnp.int32)
    is_expert = block_expert[:, None] == experts
    next_start = jnp.sum(jnp.where(is_expert, pad_end // MOE_BLOCK, 0), axis=-1)
    next_expert = jnp.sum(pad_end[None, :] <= (next_start * MOE_BLOCK)[:, None], axis=-1).astype(jnp.int32)
    next_expert = jnp.where(next_start < n_used, next_expert, -1)
    run_parity = jnp.sum((counts > 0)[None, :] & (experts < block_expert[:, None]), axis=-1).astype(jnp.int32) % 2
    xb = _dispatch(hp, slot_flat, xb_init)
    yb = _experts(xb, block_expert, n_used.reshape(1), next_expert, run_parity, layer, w1, b1, w2, b2)
    return _combine(x, gates, slot_flat, yb), xb


def kernel(x, mix_norm, ffn_norm, even_w_in, even_w_s, even_b_s, even_g_v, even_g_q, even_g_k, even_w_out, odd_w_in, odd_g_cq, odd_g_ckv, odd_w_uq, odd_w_ukv, odd_g_q, odd_g_k, odd_w_out, router_w, router_b, expert_w1, expert_b1, expert_w2, expert_b2):
    bsz, s, d = x.shape
    xt = x.reshape(bsz * s, d)
    xb = jnp.zeros((_moe_blocks(bsz * s) * MOE_BLOCK * SUBLANES, LANES), _F32)
    for l in range(mix_norm.shape[0]):
        i = l // 2
        if l % 2 == 0:
            a, qkv = _even_in(xt, mix_norm[l], even_w_in[i], even_w_s[i], even_b_s[i], even_g_v[i],
                              even_g_q[i], even_g_k[i], bsz)
            xt = _even_out(xt, a, _dilated_attention(qkv), even_w_out[i])
        else:
            q, k, v = _odd_in(xt, mix_norm[l], odd_w_in[i], odd_g_cq[i], odd_g_ckv[i], odd_w_uq[i],
                              odd_w_ukv[i], odd_g_q[i], odd_g_k[i], bsz)
            o = _flash(q, k, v, FLASH_TQ)
            xt = _proj_out(xt, o.reshape(bsz * s, -1), odd_w_out[i])
        xt, xb = _moe(xt, ffn_norm[l], router_w[l], router_b[l], l, expert_w1, expert_b1[l],
                      expert_w2, expert_b2[l], xb)
    return xt.reshape(bsz, s, d)
```

```python
import functools
import math

import jax
import jax.numpy as jnp
from jax import lax
from jax.experimental import pallas as pl
from jax.experimental.pallas import tpu as pltpu

EPS = 1e-6
NEG_INF = -1e30
A_HEADS = 4
A_HEAD_DIM = 128
A_CHUNK = 128
A_WIDTH = A_HEADS * A_HEAD_DIM
B_HEADS = 8
B_HEAD_DIM = 64
B_WIDTH = B_HEADS * B_HEAD_DIM
B_BRANCHES = ((128, 1), (512, 4), (2048, 16))
B_BLOCK = 128
C_HEADS = 8
C_NOPE = 128
C_ROPE = 64
C_V = 128
C_Q_LORA = 512
C_KV_LORA = 256
C_QK_PAD = 256
ROPE_THETA = 10000.0
N_EXPERTS = 32
TOP_K = 4
D_EXPERT = 1024
SWIGLU_ALPHA = 1.702
SWIGLU_LIMIT = 7.0
MOE_BLOCK = 512

LANES = 128
TOKEN_TILE = 256
OUT_PROJ_TILE = 512
ROW_MOVE_TILE = 512
VMEM_LIMIT = 56 * 1024 * 1024

_F32 = jnp.float32
_BF16 = jnp.bfloat16


def _params(*sem):
    return pltpu.CompilerParams(dimension_semantics=sem, vmem_limit_bytes=VMEM_LIMIT)


def _rms(x, g):
    return x * lax.rsqrt(jnp.mean(x * x, axis=-1, keepdims=True) + EPS) * g


def _gelu(x):
    return 0.5 * x * (1.0 + lax.erf(x * math.sqrt(0.5)))


def _dot(a, b):
    return jnp.dot(a, b, preferred_element_type=_F32)


def _dot_nt(a, b):
    return lax.dot_general(a, b, (((1,), (1,)), ((), ())), preferred_element_type=_F32)


def _seg_sum(x2, seg):
    hi = x2.astype(_BF16)
    lo = (x2 - hi.astype(_F32)).astype(_BF16)
    return _dot(hi, seg) + _dot(lo, seg)


def _const_spec(shape):
    nd = len(shape)
    return pl.BlockSpec(shape, lambda *_: (0,) * nd)


def _even_in_body(x_ref, g_ref, win_ref, ws_ref, bst_ref, gv_ref, gq_ref, gk_ref, seg_ref,
                  a_ref, *rest):
    qkv_refs, stage_ref = rest[:-1], rest[-1]
    x = x_ref[0]
    h = _rms(x, g_ref[...]).astype(_BF16)
    z = _dot(h, win_ref[...])
    tm = x.shape[0]
    u = _gelu(z[:, :A_WIDTH])
    vv = _gelu(z[:, A_WIDTH:2 * A_WIDTH])
    row = lax.broadcasted_iota(jnp.int32, (A_CHUNK, A_CHUNK), 0)
    col = lax.broadcasted_iota(jnp.int32, (A_CHUNK, A_CHUNK), 1)
    for hd in range(A_HEADS):
        sl = slice(hd * A_HEAD_DIM, (hd + 1) * A_HEAD_DIM)
        vh = _rms(vv[:, sl], gv_ref[hd:hd + 1, :]).astype(_BF16)
        w = jnp.where(row >= col, ws_ref[hd], 0.0).astype(_BF16)
        bias = bst_ref[:, hd:hd + 1]
        for c in range(tm // A_CHUNK):
            rs = slice(c * A_CHUNK, (c + 1) * A_CHUNK)
            mixed = _dot(w, vh[rs]) + bias
            a_ref[0, rs, sl] = (u[rs, sl] * mixed).astype(a_ref.dtype)
    o = 2 * A_WIDTH
    q = z[:, o:o + B_WIDTH]
    k = z[:, o + B_WIDTH:o + 2 * B_WIDTH]
    seg = seg_ref[...]
    inv = 1.0 / B_HEAD_DIM
    qn = q * lax.rsqrt(_seg_sum(q * q, seg) * inv + EPS) * gq_ref[...]
    kn = k * lax.rsqrt(_seg_sum(k * k, seg) * inv + EPS) * gk_ref[...]
    qkv = (qn * (B_HEAD_DIM ** -0.5), kn, z[:, o + 2 * B_WIDTH:])
    groups = B_WIDTH // LANES
    for which, val in enumerate(qkv):
        for gi in range(groups):
            stage_ref[gi] = val[:, gi * LANES:(gi + 1) * LANES]
        for bi, (_, dil) in enumerate(B_BRANCHES):
            ref = qkv_refs[3 * bi + which]
            if dil == 1:
                ref[0] = val.astype(ref.dtype)
                continue
            for r in range(dil):
                for gi in range(groups):
                    rows = stage_ref[gi, pl.ds(r, tm // dil, stride=dil), :]
                    lo = r * B_WIDTH + gi * LANES
                    ref[0, :, lo:lo + LANES] = rows.astype(ref.dtype)


def _class_view_spec(tm, dil):
    return pl.BlockSpec((1, tm // dil, dil * B_WIDTH), lambda b, j: (b, j, 0))


def _even_in(x, g, w_in, w_s, b_s, g_v, g_q, g_k, bsz):
    t, d = x.shape
    s = t // bsz
    tm = TOKEN_TILE
    seg = (jnp.arange(B_WIDTH)[:, None] // B_HEAD_DIM == jnp.arange(B_WIDTH)[None, :] // B_HEAD_DIM).astype(_BF16)
    tile = lambda w: pl.BlockSpec((1, tm, w), lambda b, j: (b, j, 0))
    qkv_specs, qkv_shapes = [], []
    for _, dil in B_BRANCHES:
        qkv_specs += [_class_view_spec(tm, dil)] * 3
        qkv_shapes += [jax.ShapeDtypeStruct((bsz, s // dil, dil * B_WIDTH), _BF16)] * 3
    outs = pl.pallas_call(
        _even_in_body,
        grid=(bsz, s // tm),
        in_specs=[tile(d), _const_spec((1, d)), _const_spec(w_in.shape), _const_spec(w_s.shape),
                  _const_spec((A_CHUNK, A_HEADS)), _const_spec(g_v.shape), _const_spec((1, B_WIDTH)),
                  _const_spec((1, B_WIDTH)), _const_spec(seg.shape)],
        out_specs=[tile(A_WIDTH)] + qkv_specs,
        out_shape=[jax.ShapeDtypeStruct((bsz, s, A_WIDTH), _BF16)] + qkv_shapes,
        scratch_shapes=[pltpu.VMEM((B_WIDTH // LANES, tm, LANES), _F32)],
        compiler_params=_params("parallel", "parallel"),
        name="even_in",
    )(x.reshape(bsz, s, d), g.reshape(1, d), w_in.astype(_BF16), w_s, b_s.T, g_v,
      jnp.tile(g_q, B_HEADS).reshape(1, B_WIDTH), jnp.tile(g_k, B_HEADS).reshape(1, B_WIDTH), seg)
    return outs[0], [outs[1 + 3 * bi:4 + 3 * bi] for bi in range(len(B_BRANCHES))]


def _dilated_body(*refs, blocks_per_class):
    nb = len(blocks_per_class)
    t = pl.program_id(1)
    for bi, n in enumerate(blocks_per_class):
        _dilated_block(*refs[5 * bi:5 * bi + 5], *refs[5 * nb + 2 * bi:5 * nb + 2 * bi + 2], t % n)


def _dilated_block(q_ref, kp_ref, kc_ref, vp_ref, vc_ref, o_ref, lse_ref, jb):
    q = q_ref[0]
    kk = jnp.concatenate([kp_ref[0], kc_ref[0]], axis=0)
    vv = jnp.concatenate([vp_ref[0], vc_ref[0]], axis=0)
    i = lax.broadcasted_iota(jnp.int32, (B_BLOCK, 2 * B_BLOCK), 0)
    c = lax.broadcasted_iota(jnp.int32, (B_BLOCK, 2 * B_BLOCK), 1)
    valid = (c >= i) & (c <= i + B_BLOCK) & ((jb > 0) | (c >= B_BLOCK))
    lane = lax.broadcasted_iota(jnp.int32, (1, LANES), 1)
    first = lane < B_HEAD_DIM
    for p in range(B_WIDTH // LANES):
        sl = slice(p * LANES, (p + 1) * LANES)
        q2, k2, v2 = q[:, sl], kk[:, sl], vv[:, sl]
        outs, lses = [], []
        for half in (first, jnp.logical_not(first)):
            s = _dot_nt(jnp.where(half, q2, jnp.zeros_like(q2)), k2)
            s = jnp.where(valid, s, NEG_INF)
            m = jnp.max(s, axis=-1, keepdims=True)
            e = jnp.exp(s - m)
            l = jnp.sum(e, axis=-1, keepdims=True)
            outs.append(_dot(e.astype(_BF16), v2) / l)
            lses.append(m + jnp.log(l))
        o_ref[0, :, sl] = jnp.where(first, outs[0], outs[1])
        lse_ref[0, :, sl] = jnp.where(first, lses[0], lses[1])


def _dilated_attention(qkv):
    bsz, s, _ = qkv[0][0].shape
    steps = s // B_BLOCK
    in_specs, out_specs, out_shapes, args, blocks_per_class = [], [], [], [], []
    for (q, k, v), (_, dil) in zip(qkv, B_BRANCHES):
        n = steps // dil
        cur = pl.BlockSpec((1, B_BLOCK, B_WIDTH), lambda b, t, n=n: (b, t % n, t // n))
        prev = pl.BlockSpec((1, B_BLOCK, B_WIDTH), lambda b, t, n=n: (b, jnp.maximum(t % n - 1, 0), t // n))
        in_specs += [cur, prev, cur, prev, cur]
        out_specs += [cur, cur]
        out_shapes += [jax.ShapeDtypeStruct(q.shape, _F32)] * 2
        args += [q, k, k, v, v]
        blocks_per_class.append(n)
    outs = pl.pallas_call(
        functools.partial(_dilated_body, blocks_per_class=tuple(blocks_per_class)),
        grid=(bsz, steps),
        in_specs=in_specs,
        out_specs=out_specs,
        out_shape=out_shapes,
        compiler_params=_params("parallel", "parallel"),
        name="dilated_attention",
    )(*args)
    return [(outs[2 * bi], outs[2 * bi + 1]) for bi in range(len(B_BRANCHES))]


def _even_out_body(x_ref, a_ref, *rest):
    nb = len(B_BRANCHES)
    branch_refs, w_ref, out_ref, stage_ref = rest[:2 * nb], rest[2 * nb], rest[2 * nb + 1], rest[2 * nb + 2]
    tm = x_ref.shape[1]

    def token_order(ref, dil, slot):
        if dil == 1:
            return ref[0]
        groups = B_WIDTH // LANES
        for r in range(dil):
            for gi in range(groups):
                lo = r * B_WIDTH + gi * LANES
                stage_ref[slot, gi, pl.ds(r, tm // dil, stride=dil), :] = ref[0, :, lo:lo + LANES]
        return jnp.concatenate([stage_ref[slot, gi] for gi in range(groups)], axis=1)

    outs = [token_order(branch_refs[2 * bi], dil, 2 * bi) for bi, (_, dil) in enumerate(B_BRANCHES)]
    lses = [token_order(branch_refs[2 * bi + 1], dil, 2 * bi + 1) for bi, (_, dil) in enumerate(B_BRANCHES)]
    m = functools.reduce(jnp.maximum, lses)
    es = [jnp.exp(l - m) for l in lses]
    b = sum(e * o for e, o in zip(es, outs)) / sum(es)
    y = _dot(a_ref[0], w_ref[:A_WIDTH, :]) + _dot(b.astype(_BF16), w_ref[A_WIDTH:, :])
    out_ref[0] = x_ref[0] + y


def _even_out(x, a, branches, w_out):
    t, d = x.shape
    bsz, s, _ = a.shape
    tm = OUT_PROJ_TILE
    tile = lambda w: pl.BlockSpec((1, tm, w), lambda b, j: (b, j, 0))
    branch_specs = []
    for _, dil in B_BRANCHES:
        branch_specs += [_class_view_spec(tm, dil)] * 2
    out = pl.pallas_call(
        _even_out_body,
        grid=(bsz, s // tm),
        in_specs=[tile(d), tile(A_WIDTH)] + branch_specs + [_const_spec(w_out.shape)],
        out_specs=tile(d),
        out_shape=jax.ShapeDtypeStruct((bsz, s, d), _F32),
        scratch_shapes=[pltpu.VMEM((2 * len(B_BRANCHES), B_WIDTH // LANES, tm, LANES), _F32)],
        compiler_params=_params("parallel", "parallel"),
        name="even_out",
    )(x.reshape(bsz, s, d), a, *[arr for pair in branches for arr in pair], w_out.astype(_BF16))
    return out.reshape(t, d)


def _rope(x, cos, sin_signed, low):
    partner = jnp.where(low, pltpu.roll(x, LANES - C_ROPE // 2, 1), pltpu.roll(x, C_ROPE // 2, 1))
    return x * cos + partner * sin_signed


def _odd_in_body(x_ref, g_ref, win_ref, gcq_ref, gckv_ref, wqn_ref, wqr_ref, wkn_ref, wv_ref,
                 gqn_ref, gqr_ref, gkn_ref, gkr_ref, cos_ref, sin_ref, q_ref, k_ref, v_ref):
    x = x_ref[0]
    h = _rms(x, g_ref[...]).astype(_BF16)
    z = _dot(h, win_ref[...])
    cq = _rms(z[:, :C_Q_LORA], gcq_ref[...]).astype(_BF16)
    ckv = _rms(z[:, C_Q_LORA:C_Q_LORA + C_KV_LORA], gckv_ref[...]).astype(_BF16)
    kr = z[:, C_Q_LORA + C_KV_LORA:]
    qn = _dot(cq, wqn_ref[...])
    qr = _dot(cq, wqr_ref[...])
    kn = _dot(ckv, wkn_ref[...])
    vv = _dot(ckv, wv_ref[...])
    cos, sin = cos_ref[...], sin_ref[...]
    lane = lax.broadcasted_iota(jnp.int32, (1, LANES), 1)
    low = (lane % C_ROPE) < (C_ROPE // 2)
    inv = 1.0 / (C_NOPE + C_ROPE)
    scale = (C_NOPE + C_ROPE) ** -0.5 * math.log2(math.e)
    kr_ss = jnp.sum(kr * kr, axis=-1, keepdims=True)
    kr_roped = _rope(kr * gkr_ref[...], cos, sin, low)
    for hd in range(C_HEADS):
        sl = slice(hd * LANES, (hd + 1) * LANES)
        qnh, qrh, knh = qn[:, sl], qr[:, sl], kn[:, sl]
        ssq = jnp.sum(qnh * qnh, axis=-1, keepdims=True) + jnp.sum(qrh * qrh, axis=-1, keepdims=True)
        rq = lax.rsqrt(ssq * inv + EPS) * scale
        q_ref[0, hd, :, :LANES] = (qnh * rq * gqn_ref[...]).astype(q_ref.dtype)
        q_ref[0, hd, :, LANES:] = (_rope(qrh * gqr_ref[...], cos, sin, low) * rq).astype(q_ref.dtype)
        ssk = jnp.sum(knh * knh, axis=-1, keepdims=True) + kr_ss
        rk = lax.rsqrt(ssk * inv + EPS)
        k_ref[0, hd, :, :LANES] = (knh * rk * gkn_ref[...]).astype(k_ref.dtype)
        k_ref[0, hd, :, LANES:] = (kr_roped * rk).astype(k_ref.dtype)
        v_ref[0, hd, :, :C_V] = vv[:, sl].astype(v_ref.dtype)
        v_ref[0, hd, :, C_V:] = jnp.ones((x.shape[0], C_V), v_ref.dtype)


def _pad_lanes(a, width):
    return jnp.pad(a, [(0, 0)] * (a.ndim - 1) + [(0, width - a.shape[-1])])


def _odd_in(x, g, w_in, g_cq, g_ckv, w_uq, w_ukv, g_q, g_k, bsz):
    t, d = x.shape
    s = t // bsz
    tm = TOKEN_TILE
    w_in_p = _pad_lanes(w_in, C_Q_LORA + C_KV_LORA + LANES).astype(_BF16)
    wq = w_uq.reshape(C_Q_LORA, C_HEADS, C_NOPE + C_ROPE)
    wqn = wq[:, :, :C_NOPE].reshape(C_Q_LORA, C_HEADS * C_NOPE).astype(_BF16)
    wqr = _pad_lanes(wq[:, :, C_NOPE:], LANES).reshape(C_Q_LORA, C_HEADS * LANES).astype(_BF16)
    wkv = w_ukv.reshape(C_KV_LORA, C_HEADS, C_NOPE + C_V)
    wkn = wkv[:, :, :C_NOPE].reshape(C_KV_LORA, C_HEADS * C_NOPE).astype(_BF16)
    wv = wkv[:, :, C_NOPE:].reshape(C_KV_LORA, C_HEADS * C_V).astype(_BF16)
    half = C_ROPE // 2
    inv_freq = ROPE_THETA ** (-jnp.arange(half, dtype=_F32) / half)
    ang = jnp.arange(s, dtype=_F32)[:, None] * inv_freq[None, :]
    cos = jnp.tile(jnp.cos(ang), (1, LANES // half))
    sin = jnp.tile(jnp.concatenate([-jnp.sin(ang), jnp.sin(ang)], axis=-1), (1, LANES // C_ROPE))
    row = lambda a: a.reshape(1, -1)
    qk_shape = jax.ShapeDtypeStruct((bsz, C_HEADS, s, C_QK_PAD), _BF16)
    head_spec = lambda w: pl.BlockSpec((1, C_HEADS, tm, w), lambda b, j: (b, 0, j, 0))
    pos_spec = pl.BlockSpec((tm, LANES), lambda b, j: (j, 0))
    consts = [row(g), w_in_p, row(g_cq), row(g_ckv), wqn, wqr, wkn, wv,
              row(g_q[:C_NOPE]), row(_pad_lanes(g_q[C_NOPE:], LANES)),
              row(g_k[:C_NOPE]), row(_pad_lanes(g_k[C_NOPE:], LANES))]
    return pl.pallas_call(
        _odd_in_body,
        grid=(bsz, s // tm),
        in_specs=[pl.BlockSpec((1, tm, d), lambda b, j: (b, j, 0))]
        + [_const_spec(c.shape) for c in consts] + [pos_spec, pos_spec],
        out_specs=[head_spec(C_QK_PAD), head_spec(C_QK_PAD), head_spec(2 * C_V)],
        out_shape=[qk_shape, qk_shape, jax.ShapeDtypeStruct((bsz, C_HEADS, s, 2 * C_V), _BF16)],
        compiler_params=_params("parallel", "parallel"),
        name="odd_in",
    )(x.reshape(bsz, s, d), *consts, cos, sin)


FLASH_ROWS = 128
FLASH_TQ = 1024


def _flash_body(q_ref, k_ref, v_ref, o_ref, s_ref, m_ref, acc_ref):
    qi = pl.program_id(2)
    tq = q_ref.shape[2]
    tk = tq
    dv = o_ref.shape[2]
    m_ref[...] = jnp.full_like(m_ref, NEG_INF)
    acc_ref[...] = jnp.zeros_like(acc_ref)

    def scores(j):
        start = pl.multiple_of(j * tk, tk)
        return _dot_nt(q_ref[0, 0], k_ref[0, 0, pl.ds(start, tk), :])

    def softmax_pv(s_all, j, diagonal):
        start = pl.multiple_of(j * tk, tk)
        v = v_ref[0, 0, pl.ds(start, tk), :]
        if diagonal:
            r = lax.broadcasted_iota(jnp.int32, s_all.shape, 0)
            col = lax.broadcasted_iota(jnp.int32, s_all.shape, 1)
            s_all = jnp.where(col <= r, s_all, NEG_INF)
        ps, scales = [], []
        for c in range(tq // FLASH_ROWS):
            rows = slice(c * FLASH_ROWS, (c + 1) * FLASH_ROWS)
            groups = [s_all[rows, g * LANES:(g + 1) * LANES] for g in range(tk // LANES)]
            m_old = m_ref[rows]
            gmax = functools.reduce(jnp.maximum, groups)
            m_new = jnp.maximum(m_old, jnp.max(gmax, axis=-1, keepdims=True))
            alpha = jnp.exp2(m_old - m_new)
            ps.append(jnp.concatenate([jnp.exp2((g - m_new).astype(_BF16)) for g in groups], axis=1))
            scales.append(jnp.concatenate([alpha] * (acc_ref.shape[1] // LANES), axis=1))
            m_ref[rows] = m_new
        pv = _dot(jnp.concatenate(ps, axis=0), v)
        acc_ref[...] = jnp.concatenate(scales, axis=0) * acc_ref[...] + pv

    s_ref[...] = scores(0)

    def full_tile(j, carry):
        s_cur = s_ref[...]
        s_next = scores(j + 1)
        softmax_pv(s_cur, j, False)
        s_ref[...] = s_next
        return carry

    lax.fori_loop(0, qi, full_tile, 0)
    softmax_pv(s_ref[...], qi, True)
    acc = acc_ref[...]
    o_ref[0] = (acc[:, :dv] / acc[:, dv:2 * dv]).astype(o_ref.dtype)


def _flash(q, k, v, tq):
    bsz, nh, s, dq = q.shape
    dv2 = v.shape[-1]
    dv = dv2 // 2
    whole = lambda w: pl.BlockSpec((1, 1, s, w), lambda b, h, i: (b, h, 0, 0))
    return pl.pallas_call(
        _flash_body,
        grid=(bsz, nh, s // tq),
        in_specs=[pl.BlockSpec((1, 1, tq, dq), lambda b, h, i: (b, h, i, 0)), whole(dq), whole(dv2)],
        out_specs=pl.BlockSpec((1, tq, dv), lambda b, h, i: (b, i, h)),
        out_shape=jax.ShapeDtypeStruct((bsz, s, nh * dv), _BF16),
        scratch_shapes=[pltpu.VMEM((tq, tq), _F32), pltpu.VMEM((tq, LANES), _F32), pltpu.VMEM((tq, dv2), _F32)],
        compiler_params=_params("parallel", "parallel", "arbitrary"),
        name="mla_flash",
    )(q, k, v)


def _proj_out_body(x_ref, o_ref, w_ref, out_ref):
    out_ref[...] = x_ref[...] + _dot(o_ref[...], w_ref[...])


def _proj_out(x, o, w_out):
    t, d = x.shape
    tm = OUT_PROJ_TILE
    tile = lambda w: pl.BlockSpec((tm, w), lambda i: (i, 0))
    return pl.pallas_call(
        _proj_out_body,
        grid=(t // tm,),
        in_specs=[tile(d), tile(o.shape[1]), _const_spec(w_out.shape)],
        out_specs=tile(d),
        out_shape=jax.ShapeDtypeStruct((t, d), _F32),
        compiler_params=_params("parallel"),
        name="odd_out",
    )(x, o, w_out.astype(_BF16))


SUBLANES = 8


def _store_row_tiles(ref, val, lead=()):
    n = val.shape[0]
    for j in range(SUBLANES):
        ref[lead + (pl.ds(j, n, stride=SUBLANES), slice(None))] = val[:, j * LANES:(j + 1) * LANES]


def _load_row_tile_column(ref, n, j, lead=()):
    return ref[lead + (pl.ds(j, n, stride=SUBLANES), slice(None))]


def _row_tile(ref, row):
    return ref.at[pl.ds(pl.multiple_of(row * SUBLANES, SUBLANES), SUBLANES)]


def _router_body(x_ref, g_ref, wt_ref, b_ref, hp_ref, meta_ref, gate_ref, cnt_ref, base_ref):
    @pl.when(pl.program_id(0) == 0)
    def _():
        base_ref[...] = jnp.zeros_like(base_ref)

    h = _rms(x_ref[...], g_ref[...])
    tm = h.shape[0]
    _store_row_tiles(hp_ref, h)
    w = wt_ref[...]
    h_hi, w_hi = h.astype(_BF16), w.astype(_BF16)
    h_lo = (h - h_hi.astype(_F32)).astype(_BF16)
    w_lo = (w - w_hi.astype(_F32)).astype(_BF16)
    logits = _dot_nt(w_hi, h_hi) + _dot_nt(w_hi, h_lo) + _dot_nt(w_lo, h_hi) + b_ref[...]

    expert = lax.broadcasted_iota(jnp.int32, logits.shape, 0)
    hits, sels, tops = [], [], []
    work = logits
    for k in range(TOP_K):
        m = jnp.max(work, axis=0, keepdims=True)
        sel = jnp.min(jnp.where(work == m, expert, N_EXPERTS), axis=0, keepdims=True)
        hit = expert == sel
        work = jnp.where(hit, -jnp.inf, work)
        hits.append(hit)
        sels.append(sel)
        tops.append(m)
    r = lax.broadcasted_iota(jnp.int32, (tm, tm), 0)
    c = lax.broadcasted_iota(jnp.int32, (tm, tm), 1)
    earlier_tokens = (r < c).astype(_BF16)
    stacked = jnp.concatenate([hit.astype(_BF16) for hit in hits], axis=0)
    cum = _dot(stacked, earlier_tokens)
    base = base_ref[...][:, :1]
    ranks = []
    for k in range(TOP_K):
        onehot = hits[k].astype(_F32)
        before = cum[k * N_EXPERTS:(k + 1) * N_EXPERTS] + base
        ranks.append(jnp.sum(onehot * before, axis=0, keepdims=True).astype(jnp.int32))
        base = base + jnp.sum(onehot, axis=1, keepdims=True)
    base_ref[...] = jnp.broadcast_to(base, base_ref.shape)
    cnt_ref[...] = jnp.broadcast_to(base, cnt_ref.shape)
    exps = [jnp.ones_like(tops[0])] + [jnp.exp(tops[k] - tops[0]) for k in range(1, TOP_K)]
    denom = functools.reduce(lambda a_, b_: a_ + b_, exps)
    meta_ref[...] = jnp.concatenate(sels + ranks, axis=0)
    gate_ref[...] = jnp.concatenate([e / denom for e in exps] + [jnp.zeros_like(denom)] * TOP_K, axis=0)


def _router(x, g, w_r, b_r):
    t, d = x.shape
    tm = TOKEN_TILE
    rows = 2 * TOP_K
    lane_tile = pl.BlockSpec((rows, tm), lambda i: (0, i))
    hp, meta, gates, counts = pl.pallas_call(
        _router_body,
        grid=(t // tm,),
        in_specs=[pl.BlockSpec((tm, d), lambda i: (i, 0)), _const_spec((1, d)), _const_spec((N_EXPERTS, d)),
                  _const_spec((N_EXPERTS, 1))],
        out_specs=[pl.BlockSpec((tm * SUBLANES, LANES), lambda i: (i, 0)), lane_tile, lane_tile,
                   _const_spec((N_EXPERTS, LANES))],
        out_shape=[jax.ShapeDtypeStruct((t * SUBLANES, LANES), _F32), jax.ShapeDtypeStruct((rows, t), jnp.int32),
                   jax.ShapeDtypeStruct((rows, t), _F32), jax.ShapeDtypeStruct((N_EXPERTS, LANES), _F32)],
        scratch_shapes=[pltpu.VMEM((N_EXPERTS, LANES), _F32)],
        compiler_params=_params("arbitrary"),
        name="moe_router",
    )(x, g.reshape(1, d), w_r.T, b_r.reshape(N_EXPERTS, 1))
    return hp, meta, gates, counts[:, 0]


def _dispatch_body(slot_ref, hp_ref, xb_in_ref, xb_ref, sem):
    del xb_in_ref
    tm = hp_ref.shape[0] // SUBLANES

    def start(r, carry):
        for k in range(TOP_K):
            s = slot_ref[r * TOP_K + k]
            pltpu.make_async_copy(_row_tile(hp_ref, r), _row_tile(xb_ref, s), sem).start(priority=k % 2)
        return carry

    lax.fori_loop(0, tm, start, 0, unroll=8)
    for _ in range(TOP_K):
        pltpu.make_async_copy(hp_ref, xb_ref.at[pl.ds(0, tm * SUBLANES)], sem).wait()


def _dispatch(hp, slot_flat, xb_init):
    t = hp.shape[0] // SUBLANES
    tm = ROW_MOVE_TILE
    return pl.pallas_call(
        _dispatch_body,
        grid=(t // tm,),
        in_specs=[pl.BlockSpec((tm * TOP_K,), lambda i: (i,), memory_space=pltpu.SMEM),
                  pl.BlockSpec((tm * SUBLANES, LANES), lambda i: (i, 0)),
                  pl.BlockSpec(memory_space=pl.ANY)],
        out_specs=pl.BlockSpec(memory_space=pl.ANY),
        out_shape=jax.ShapeDtypeStruct(xb_init.shape, hp.dtype),
        scratch_shapes=[pltpu.SemaphoreType.DMA(())],
        input_output_aliases={2: 0},
        compiler_params=_params("arbitrary"),
        name="moe_dispatch",
    )(slot_flat, hp, xb_init)


def _expert_body(be_ref, nu_ref, nx_ref, par_ref, xb_ref, w1_hbm, b1g_ref, b1l_ref, w2_hbm, b2_ref, perm_ref,
                 yb_ref, w1_buf, w2_buf, w1g_s, w1l_s, w2_s, sems, *, layer):
    b = pl.program_id(0)
    live = b < nu_ref[0]
    changed = (b == 0) | (be_ref[b] != be_ref[jnp.maximum(b - 1, 0)])

    def fetch(e, slot):
        return (pltpu.make_async_copy(w1_hbm.at[layer, e], w1_buf.at[slot], sems.at[slot, 0]),
                pltpu.make_async_copy(w2_hbm.at[layer, e], w2_buf.at[slot], sems.at[slot, 1]))

    @pl.when(b == 0)
    def _():
        for copy in fetch(be_ref[0], 0):
            copy.start()

    @pl.when(live & changed)
    def _():
        slot = par_ref[b]
        for copy in fetch(be_ref[b], slot):
            copy.wait()

        @pl.when(nx_ref[b] >= 0)
        def _():
            for copy in fetch(nx_ref[b], 1 - slot):
                copy.start()

        group = perm_ref.shape[0]
        for c in range(w1_buf.shape[2] // group):
            wc = w1_buf[slot, :, c * group:(c + 1) * group].astype(_BF16)
            d = _dot(wc, perm_ref[...]).astype(_BF16)
            w1g_s[:, c * (group // 2):(c + 1) * (group // 2)] = d[:, :group // 2]
            w1l_s[:, c * (group // 2):(c + 1) * (group // 2)] = d[:, group // 2:]
        w2_s[...] = w2_buf[slot].astype(_BF16)

    @pl.when(live)
    def _():
        x = jnp.concatenate([_load_row_tile_column(xb_ref, MOE_BLOCK, j) for j in range(SUBLANES)],
                            axis=1).astype(_BF16)
        glu = _dot(x, w1g_s[...]) + b1g_ref[0]
        lin = _dot(x, w1l_s[...]) + b1l_ref[0]
        glu = jnp.minimum(glu, SWIGLU_LIMIT)
        lin = jnp.clip(lin, -SWIGLU_LIMIT, SWIGLU_LIMIT)
        act = glu * jax.nn.sigmoid(SWIGLU_ALPHA * glu) * (lin + 1.0)
        _store_row_tiles(yb_ref, _dot(act.astype(_BF16), w2_s[...]) + b2_ref[0])

    @pl.when(jnp.logical_not(live))
    def _():
        yb_ref[...] = jnp.zeros_like(yb_ref)


def _experts(xb, block_expert, n_used, next_expert, run_parity, layer, w1, b1, w2, b2):
    n_rows = xb.shape[0] // SUBLANES
    n_blocks = n_rows // MOE_BLOCK
    _, n_e, d, two_f = w1.shape
    row_spec = lambda idx: pl.BlockSpec((MOE_BLOCK * SUBLANES, LANES), lambda b, be, nu, nx, par: (idx(b, nu), 0))
    f = two_f // 2
    group = 2 * LANES
    j = jnp.arange(group)
    src = jnp.where(j < LANES, 2 * j, 2 * (j - LANES) + 1)
    perm = (jnp.arange(group)[:, None] == src[None, :]).astype(_BF16)
    b1p = b1.reshape(n_e, 1, f, 2)
    e_spec = lambda shape: pl.BlockSpec((1,) + shape, lambda b, be, nu, nx, par: (be[b],) + (0,) * len(shape))
    hbm = pl.BlockSpec(memory_space=pl.ANY)
    grid_spec = pltpu.PrefetchScalarGridSpec(
        num_scalar_prefetch=4,
        grid=(n_blocks,),
        in_specs=[row_spec(lambda b, nu: jnp.minimum(b, nu[0] - 1)),
                  hbm, e_spec((1, f)), e_spec((1, f)), hbm, e_spec((1, d)),
                  pl.BlockSpec((group, group), lambda b, be, nu, nx, par: (0, 0))],
        out_specs=row_spec(lambda b, nu: b),
        scratch_shapes=[pltpu.VMEM((2, d, two_f), w1.dtype), pltpu.VMEM((2, f, d), w2.dtype),
                        pltpu.VMEM((d, f), _BF16), pltpu.VMEM((d, f), _BF16), pltpu.VMEM((f, d), _BF16),
                        pltpu.SemaphoreType.DMA((2, 2))],
    )
    return pl.pallas_call(
        functools.partial(_expert_body, layer=layer),
        grid_spec=grid_spec,
        out_shape=jax.ShapeDtypeStruct((n_rows * SUBLANES, LANES), _F32),
        compiler_params=_params("arbitrary"),
        name="moe_experts",
    )(block_expert, n_used, next_expert, run_parity, xb, w1, b1p[..., 0], b1p[..., 1], w2,
      b2.reshape(n_e, 1, d), perm)


def _combine_body(slot_ref, next_slot_ref, x_ref, gate_ref, yb_ref, out_ref, rows_ref, sems):
    i = pl.program_id(0)
    tm = x_ref.shape[0]
    groups = tm // SUBLANES

    def start_group(slots, p, g):
        for u in range(SUBLANES):
            r = g * SUBLANES + u
            for k in range(TOP_K):
                s = slots[r * TOP_K + k]
                pltpu.make_async_copy(_row_tile(yb_ref, s), _row_tile(rows_ref.at[p, k], r),
                                      sems.at[p]).start(priority=k % 2)

    def sum_group(p, g):
        t0 = pl.multiple_of(g * SUBLANES, SUBLANES)
        tok = pl.ds(t0, SUBLANES)
        gates = [jnp.broadcast_to(gate_ref[tok, k:k + 1], (SUBLANES, LANES)) for k in range(TOP_K)]
        for j in range(SUBLANES):
            cols = slice(j * LANES, (j + 1) * LANES)
            acc = x_ref[tok, cols]
            for k in range(TOP_K):
                acc = acc + gates[k] * rows_ref[p, k, pl.ds(t0 * SUBLANES + j, SUBLANES, stride=SUBLANES), :]
            out_ref[tok, cols] = acc

    def loop(body):
        def trip(g, carry):
            body(g)
            return carry

        lax.fori_loop(0, groups, trip, 0)

    @pl.when(i == 0)
    def _():
        loop(lambda g: start_group(slot_ref, 0, g))

    for p in range(2):
        @pl.when(i % 2 == p)
        def _():
            for k in range(TOP_K):
                pltpu.make_async_copy(yb_ref.at[pl.ds(0, tm * SUBLANES)], rows_ref.at[p, k], sems.at[p]).wait()

            @pl.when(i + 1 < pl.num_programs(0))
            def _():
                def both(g):
                    start_group(next_slot_ref, 1 - p, g)
                    sum_group(p, g)

                loop(both)

            @pl.when(i + 1 == pl.num_programs(0))
            def _():
                loop(lambda g: sum_group(p, g))


def _combine(x, gates, slot_flat, yb):
    t, d = x.shape
    tm = ROW_MOVE_TILE
    last = t // tm - 1
    return pl.pallas_call(
        _combine_body,
        grid=(t // tm,),
        in_specs=[pl.BlockSpec((tm * TOP_K,), lambda i: (i,), memory_space=pltpu.SMEM),
                  pl.BlockSpec((tm * TOP_K,), lambda i: (jnp.minimum(i + 1, last),), memory_space=pltpu.SMEM),
                  pl.BlockSpec((tm, d), lambda i: (i, 0)),
                  pl.BlockSpec((tm, LANES), lambda i: (i, 0)),
                  pl.BlockSpec(memory_space=pl.ANY)],
        out_specs=pl.BlockSpec((tm, d), lambda i: (i, 0)),
        out_shape=jax.ShapeDtypeStruct((t, d), _F32),
        scratch_shapes=[pltpu.VMEM((2, TOP_K, tm * SUBLANES, LANES), _F32), pltpu.SemaphoreType.DMA((2,))],
        compiler_params=_params("arbitrary"),
        name="moe_combine",
    )(slot_flat, slot_flat, x, gates, yb)


def _moe_blocks(t):
    return -(-(t * TOP_K + N_EXPERTS * (MOE_BLOCK - 1)) // MOE_BLOCK)


def _moe(x, g, w_r, b_r, layer, w1, b1, w2, b2, xb_init):
    t = x.shape[0]
    n_blocks = _moe_blocks(t)
    hp, meta, gates_t, counts = _router(x, g, w_r, b_r)
    counts = counts.astype(jnp.int32)
    padded = (counts + MOE_BLOCK - 1) // MOE_BLOCK * MOE_BLOCK
    pad_end = jnp.cumsum(padded)
    pad_start = pad_end - padded
    experts = jnp.arange(N_EXPERTS, dtype=jnp.int32)
    start_of = jnp.sum(jnp.where(meta[:TOP_K, :, None] == experts, pad_start, 0), axis=-1)
    slot_flat = (start_of + meta[TOP_K:]).T.reshape(-1).astype(jnp.int32)
    gates = _pad_lanes(gates_t[:TOP_K].T, LANES)
    n_used = (pad_end[-1] // MOE_BLOCK).astype(jnp.int32)
    block_id = jnp.minimum(jnp.arange(n_blocks, dtype=jnp.int32), n_used - 1)
    block_expert = jnp.sum(pad_end[None, :] <= (block_id * MOE_BLOCK)[:, None], axis=-1).astype(jnp.int32)
    is_expert = block_expert[:, None] == experts
    next_start = jnp.sum(jnp.where(is_expert, pad_end // MOE_BLOCK, 0), axis=-1)
    next_expert = jnp.sum(pad_end[None, :] <= (next_start * MOE_BLOCK)[:, None], axis=-1).astype(jnp.int32)
    next_expert = jnp.where(next_start < n_used, next_expert, -1)
    run_parity = jnp.sum((counts > 0)[None, :] & (experts < block_expert[:, None]), axis=-1).astype(jnp.int32) % 2
    xb = _dispatch(hp, slot_flat, xb_init)
    yb = _experts(xb, block_expert, n_used.reshape(1), next_expert, run_parity, layer, w1, b1, w2, b2)
    return _combine(x, gates, slot_flat, yb), xb


def kernel(x, mix_norm, ffn_norm, even_w_in, even_w_s, even_b_s, even_g_v, even_g_q, even_g_k, even_w_out, odd_w_in, odd_g_cq, odd_g_ckv, odd_w_uq, odd_w_ukv, odd_g_q, odd_g_k, odd_w_out, router_w, router_b, expert_w1, expert_b1, expert_w2, expert_b2):
    bsz, s, d = x.shape
    xt = x.reshape(bsz * s, d)
    xb = jnp.zeros((_moe_blocks(bsz * s) * MOE_BLOCK * SUBLANES, LANES), _F32)
    for l in range(mix_norm.shape[0]):
        i = l // 2
        if l % 2 == 0:
            a, qkv = _even_in(xt, mix_norm[l], even_w_in[i], even_w_s[i], even_b_s[i], even_g_v[i],
                              even_g_q[i], even_g_k[i], bsz)
            xt = _even_out(xt, a, _dilated_attention(qkv), even_w_out[i])
        else:
            q, k, v = _odd_in(xt, mix_norm[l], odd_w_in[i], odd_g_cq[i], odd_g_ckv[i], odd_w_uq[i],
                              odd_w_ukv[i], odd_g_q[i], odd_g_k[i], bsz)
            o = _flash(q, k, v, FLASH_TQ)
            xt = _proj_out(xt, o.reshape(bsz * s, -1), odd_w_out[i])
        xt, xb = _moe(xt, ffn_norm[l], router_w[l], router_b[l], l, expert_w1, expert_b1[l],
                      expert_w2, expert_b2[l], xb)
    return xt.reshape(bsz, s, d)
```
